```python
import jax, jax.numpy as jnp
from jax import lax
import numpy as np

D_MODEL = 1024
BATCH = 16
SEQ = 2048
DEPTH = 1

HEAD_DIM = 64
ATTN_HEADS = 8
ATTN_WIDTH = ATTN_HEADS * HEAD_DIM
ROPE_DIM = HEAD_DIM // 4
ROPE_THETA = 500000.0
DILATED_PATTERNS = ((128, 1), (512, 4), (2048, 16))
HG_HEADS = 4
HG_EXPAND = 128
HG_VDIM = 128
HG_KDIM = HG_HEADS * HG_EXPAND
HG_WIDTH = HG_HEADS * HG_VDIM
HG_CHUNK = 64
MIX_WIDTH = ATTN_WIDTH + HG_WIDTH
IN_SIZES = (ATTN_WIDTH, ATTN_WIDTH, ATTN_WIDTH, HG_KDIM, HG_KDIM, HG_WIDTH, HG_WIDTH)
IN_COLS = sum(IN_SIZES)
D_FF = 2816
EPS = 1e-6
NEG_INF = -1e30

kernel_name = 'hymba_longnet_hgrn2_macaron_block'


def _rms_norm(x, w):
    xf = x.astype(jnp.float32)
    y = xf * lax.rsqrt(jnp.mean(xf * xf, axis=-1, keepdims=True) + EPS)
    return (y * w.astype(jnp.float32)).astype(x.dtype)


def _swiglu(h, w1, w3, w2):
    return (jax.nn.silu(h @ w1) * (h @ w3)) @ w2


def _rope_tables(s):
    inv = ROPE_THETA ** (-jnp.arange(0, ROPE_DIM, 2, dtype=jnp.float32) / ROPE_DIM)
    ang = jnp.arange(s, dtype=jnp.float32)[:, None] * inv[None, :]
    return jnp.cos(ang), jnp.sin(ang)


def _partial_rope(x, cos, sin):
    half = ROPE_DIM // 2
    xf = x.astype(jnp.float32)
    x1, x2, rest = xf[..., :half], xf[..., half:ROPE_DIM], xf[..., ROPE_DIM:]
    c, s_ = cos[None, :, None, :], sin[None, :, None, :]
    out = jnp.concatenate([x1 * c - x2 * s_, x1 * s_ + x2 * c, rest], axis=-1)
    return out.astype(x.dtype)


def _dilated_branch(q, k, v, window, dilation):
    b, s, h, dh = q.shape
    w = window // dilation
    l = s // dilation
    nb = -(-l // w)
    lp = nb * w
    bb = b * dilation

    def to_sub(t):
        t = t.reshape(b, l, dilation, h, dh).transpose(0, 2, 1, 3, 4).reshape(bb, l, h, dh)
        return jnp.pad(t, ((0, 0), (0, lp - l), (0, 0), (0, 0)))

    def windows(t):
        t = jnp.pad(t, ((0, 0), (w, 0), (0, 0), (0, 0))).reshape(bb, nb + 1, w, h, dh)
        return jnp.concatenate([t[:, :-1], t[:, 1:]], axis=2)

    qb = to_sub(q).reshape(bb, nb, w, h, dh)
    kw, vw = windows(to_sub(k)), windows(to_sub(v))
    scores = jnp.einsum('bnqhd,bnkhd->bnhqk', qb, kw,
                        preferred_element_type=jnp.float32) * (dh ** -0.5)
    qi = jnp.arange(w)[:, None]
    kj = jnp.arange(2 * w)[None, :]
    dist = qi + w - kj
    band = (dist >= 0) & (dist <= w)
    valid = band[None] & ((jnp.arange(nb)[:, None, None] > 0) | (kj >= w)[None])
    scores = jnp.where(valid[None, :, None], scores, NEG_INF)
    m = jnp.max(scores, axis=-1, keepdims=True)
    p = jnp.exp(scores - m)
    den = jnp.sum(p, axis=-1, keepdims=True)
    out = jnp.einsum('bnhqk,bnkhd->bnqhd', p / den, vw.astype(jnp.float32))
    lse = (m + jnp.log(den))[..., 0]
    out = (out.reshape(bb, lp, h, dh)[:, :l]
           .reshape(b, dilation, l, h, dh).transpose(0, 2, 1, 3, 4).reshape(b, s, h, dh))
    lse = (lse.transpose(0, 1, 3, 2).reshape(bb, lp, h)[:, :l]
           .reshape(b, dilation, l, h).transpose(0, 2, 1, 3).reshape(b, s, h))
    return out, lse


def _dilated_attention(q, k, v):
    outs, lses = [], []
    for window, dilation in DILATED_PATTERNS:
        o, lse = _dilated_branch(q, k, v, window, dilation)
        outs.append(o)
        lses.append(lse)
    wts = jax.nn.softmax(jnp.stack(lses), axis=0)
    return jnp.einsum('pbsh,pbshd->bshd', wts, jnp.stack(outs))


def _hgrn2(q, f_pre, inp, lb):
    b, s, h, n = q.shape
    dv = inp.shape[-1]
    c = HG_CHUNK
    nc = s // c
    lb = lb.reshape(h, n).astype(jnp.float32)
    f = lb + (1.0 - lb) * jax.nn.sigmoid(f_pre.astype(jnp.float32))
    g = jnp.log(f)
    k = 1.0 - f

    def chunks(t):
        return t.reshape(b, nc, c, h, t.shape[-1]).transpose(0, 3, 1, 2, 4)

    qc, kc, gc, vc = chunks(q.astype(jnp.float32)), chunks(k), chunks(g), chunks(inp.astype(jnp.float32))
    G = jnp.cumsum(gc, axis=3)
    G_last = G[:, :, :, -1:]
    q_dec = qc * jnp.exp(G)
    att = jnp.einsum('bhntk,bhnsk->bhnts', q_dec, kc * jnp.exp(-G))
    att = jnp.where(jnp.tril(jnp.ones((c, c), dtype=bool)), att, 0.0)
    intra = jnp.einsum('bhnts,bhnsv->bhntv', att, vc)
    chunk_state = jnp.einsum('bhnsk,bhnsv->bhnkv', kc * jnp.exp(G_last - G), vc)
    decay = jnp.exp(G_last[:, :, :, 0])

    def step(state, xs):
        dec, upd = xs
        return dec[..., None] * state + upd, state

    _, prev = lax.scan(step, jnp.zeros((b, h, n, dv), jnp.float32),
                       (jnp.moveaxis(decay, 2, 0), jnp.moveaxis(chunk_state, 2, 0)))
    prev = jnp.moveaxis(prev, 0, 2)
    inter = jnp.einsum('bhntk,bhnkv->bhntv', q_dec, prev)
    return (intra + inter).transpose(0, 2, 3, 1, 4).reshape(b, s, h, dv)


def setup_inputs(seed: int = 0) -> dict:
    key = jax.random.key(seed)
    ks = jax.random.split(key, 20)
    f32 = jnp.float32

    def nrm(k, shape, fan_in):
        return jax.random.normal(k, shape, f32) * (fan_in ** -0.5)

    def gain(k, shape):
        return 1.0 + 0.02 * jax.random.normal(k, shape, f32)

    return {
        'x': jax.random.normal(ks[0], (BATCH, SEQ, D_MODEL), f32),
        'ffn1_norm': gain(ks[1], (DEPTH, D_MODEL)),
        'ffn1_w1': nrm(ks[2], (DEPTH, D_MODEL, D_FF), D_MODEL),
        'ffn1_w3': nrm(ks[3], (DEPTH, D_MODEL, D_FF), D_MODEL),
        'ffn1_w2': nrm(ks[4], (DEPTH, D_FF, D_MODEL), D_FF),
        'mix_norm': gain(ks[5], (DEPTH, D_MODEL)),
        'w_in': nrm(ks[6], (DEPTH, D_MODEL, IN_COLS), D_MODEL),
        'q_norm': gain(ks[7], (DEPTH, HEAD_DIM)),
        'k_norm': gain(ks[8], (DEPTH, HEAD_DIM)),
        'hg_lb_logits': 0.1 * jax.random.normal(ks[9], (DEPTH + 1, HG_KDIM), f32),
        'hg_out_norm': gain(ks[10], (DEPTH, HG_VDIM)),
        'w_out': nrm(ks[11], (DEPTH, MIX_WIDTH, D_MODEL), MIX_WIDTH),
        'ffn2_norm': gain(ks[12], (DEPTH, D_MODEL)),
        'ffn2_w1': nrm(ks[13], (DEPTH, D_MODEL, D_FF), D_MODEL),
        'ffn2_w3': nrm(ks[14], (DEPTH, D_MODEL, D_FF), D_MODEL),
        'ffn2_w2': nrm(ks[15], (DEPTH, D_FF, D_MODEL), D_FF),
    }


def reference(x, ffn1_norm, ffn1_w1, ffn1_w3, ffn1_w2, mix_norm, w_in, q_norm, k_norm,
              hg_lb_logits, hg_out_norm, w_out, ffn2_norm, ffn2_w1, ffn2_w3, ffn2_w2):
    b, s, _ = x.shape
    cos, sin = _rope_tables(s)
    lower_bounds = jnp.cumsum(jax.nn.softmax(hg_lb_logits.astype(jnp.float32), axis=0), axis=0)
    split_at = list(np.cumsum(IN_SIZES)[:-1])
    for layer in range(DEPTH):
        x = x + 0.5 * _swiglu(_rms_norm(x, ffn1_norm[layer]),
                              ffn1_w1[layer], ffn1_w3[layer], ffn1_w2[layer])
        h = _rms_norm(x, mix_norm[layer])
        proj = h @ w_in[layer]
        aq, ak, av, hq, hf, hi, hg = jnp.split(proj, split_at, axis=-1)
        aq = _partial_rope(_rms_norm(aq.reshape(b, s, ATTN_HEADS, HEAD_DIM), q_norm[layer]), cos, sin)
        ak = _partial_rope(_rms_norm(ak.reshape(b, s, ATTN_HEADS, HEAD_DIM), k_norm[layer]), cos, sin)
        av = av.reshape(b, s, ATTN_HEADS, HEAD_DIM)
        attn_out = _dilated_attention(aq, ak, av).astype(x.dtype).reshape(b, s, ATTN_WIDTH)
        rec = _hgrn2(hq.reshape(b, s, HG_HEADS, HG_EXPAND), hf.reshape(b, s, HG_HEADS, HG_EXPAND),
                     hi.reshape(b, s, HG_HEADS, HG_VDIM), lower_bounds[layer]).astype(x.dtype)
        rec = _rms_norm(rec, hg_out_norm[layer]) * jax.nn.silu(hg.reshape(b, s, HG_HEADS, HG_VDIM))
        mixed = jnp.concatenate([attn_out, rec.reshape(b, s, HG_WIDTH)], axis=-1)
        x = x + mixed @ w_out[layer]
        x = x + 0.5 * _swiglu(_rms_norm(x, ffn2_norm[layer]),
                              ffn2_w1[layer], ffn2_w3[layer], ffn2_w2[layer])
    return x
```

```python
import functools

import jax
import jax.numpy as jnp
from jax import lax
from jax.experimental import pallas as pl
from jax.experimental.pallas import tpu as pltpu

F32 = jnp.float32
BF16 = jnp.bfloat16

EPS = 1e-6
NEG_INF = -1e30
HEAD_DIM = 64
ATTN_HEADS = 8
ATTN_WIDTH = ATTN_HEADS * HEAD_DIM
ROPE_DIM = HEAD_DIM // 4
ROPE_THETA = 500000.0
DILATED_PATTERNS = ((128, 1), (512, 4), (2048, 16))
HG_HEADS = 4
HG_DIM = 128
HG_CHUNK = 64
HG_WIDTH = HG_HEADS * HG_DIM

LANES = 128
ATTN_BLOCK = 128
VMEM_LIMIT = 56 * 1024 * 1024


def _const_spec(shape):
    nd = len(shape)
    return pl.BlockSpec(shape, lambda *_: (0,) * nd, pipeline_mode=pl.Buffered(1))


def _rms_norm_rows(x, w):
    return x * lax.rsqrt(jnp.mean(x * x, axis=-1, keepdims=True) + EPS) * w


def _silu(a):
    return a * (1.0 / (1.0 + jnp.exp(-a)))


def _swiglu_half_step(x, nw, w1_ref, w3_ref, w2_ref):
    h = _rms_norm_rows(x, nw).astype(BF16)
    a = jnp.dot(h, w1_ref[...], preferred_element_type=F32)
    b = jnp.dot(h, w3_ref[...], preferred_element_type=F32)
    g = (_silu(a) * b).astype(BF16)
    y = jnp.dot(g, w2_ref[...], preferred_element_type=F32)
    return x + 0.5 * y


def _ffn1_kernel(x_ref, nw_ref, w1_ref, w3_ref, w2_ref, o_ref):
    o_ref[...] = _swiglu_half_step(x_ref[...], nw_ref[...], w1_ref, w3_ref, w2_ref)


def _ffn1(x2d, nw, w1, w3, w2, tm):
    n, d = x2d.shape
    f = w1.shape[1]
    return pl.pallas_call(
        _ffn1_kernel,
        grid=(n // tm,),
        in_specs=[
            pl.BlockSpec((tm, d), lambda i: (i, 0)),
            _const_spec((1, d)),
            _const_spec((d, f)),
            _const_spec((d, f)),
            _const_spec((f, d)),
        ],
        out_specs=pl.BlockSpec((tm, d), lambda i: (i, 0)),
        out_shape=jax.ShapeDtypeStruct((n, d), F32),
        compiler_params=pltpu.CompilerParams(
            dimension_semantics=("arbitrary",), vmem_limit_bytes=VMEM_LIMIT),
        name="ffn1",
    )(x2d, nw, w1, w3, w2)


def _head_norm_rope(t, seg_ref, w, cos, sin_lo, sin_hi, scale):
    sq = t * t
    hi = sq.astype(BF16)
    lo = (sq - hi.astype(F32)).astype(BF16)
    seg = seg_ref[...]
    ms = (jnp.dot(hi, seg, preferred_element_type=F32)
          + jnp.dot(lo, seg, preferred_element_type=F32)) * (1.0 / HEAD_DIM)
    y = t * lax.rsqrt(ms + EPS) * w
    outs = []
    for c in range(ATTN_WIDTH // LANES):
        yc = y[:, c * LANES:(c + 1) * LANES]
        from_hi = pltpu.roll(yc, LANES - ROPE_DIM // 2, axis=1)
        from_lo = pltpu.roll(yc, ROPE_DIM // 2, axis=1)
        outs.append((yc * cos + from_hi * sin_lo + from_lo * sin_hi) * scale)
    return jnp.concatenate(outs, axis=1)


def _in_proj_kernel(x_ref, nw_ref, w_ref, seg_ref, qw_ref, kw_ref, cos_ref, slo_ref, shi_ref,
                    q_ref, k_ref, v_ref, hq_ref, hf_ref, hi_ref, hg_ref):
    h = _rms_norm_rows(x_ref[...], nw_ref[...]).astype(BF16)
    p = jnp.dot(h, w_ref[...], preferred_element_type=F32)
    a = ATTN_WIDTH
    cos, slo, shi = cos_ref[...], slo_ref[...], shi_ref[...]
    q_ref[...] = _head_norm_rope(p[:, 0:a], seg_ref, qw_ref[...], cos, slo, shi,
                                 HEAD_DIM ** -0.5).astype(BF16)
    k_ref[...] = _head_norm_rope(p[:, a:2 * a], seg_ref, kw_ref[...], cos, slo, shi,
                                 1.0).astype(BF16)
    v_ref[...] = p[:, 2 * a:3 * a].astype(BF16)
    o = 3 * a
    hq_ref[...] = p[:, o:o + HG_WIDTH].astype(BF16)
    hf_ref[...] = p[:, o + HG_WIDTH:o + 2 * HG_WIDTH]
    hi_ref[...] = p[:, o + 2 * HG_WIDTH:o + 3 * HG_WIDTH].astype(BF16)
    hg_ref[...] = p[:, o + 3 * HG_WIDTH:o + 4 * HG_WIDTH].astype(BF16)


def _in_proj(x2d, nw, w_in, seg, qw, kw, cos, slo, shi, tm, seq):
    n, d = x2d.shape
    cols = w_in.shape[1]
    a = ATTN_WIDTH
    n_pos = seq // tm
    row = lambda i: (i, 0)
    pos = lambda i: (i % n_pos, 0)
    bf = lambda w: jax.ShapeDtypeStruct((n, w), BF16)
    return pl.pallas_call(
        _in_proj_kernel,
        grid=(n // tm,),
        in_specs=[
            pl.BlockSpec((tm, d), row),
            _const_spec((1, d)),
            _const_spec((d, cols)),
            _const_spec((a, a)),
            _const_spec((1, a)),
            _const_spec((1, a)),
            pl.BlockSpec((tm, LANES), pos),
            pl.BlockSpec((tm, LANES), pos),
            pl.BlockSpec((tm, LANES), pos),
        ],
        out_specs=[pl.BlockSpec((tm, a), row)] * 3 + [pl.BlockSpec((tm, HG_WIDTH), row)] * 4,
        out_shape=[bf(a), bf(a), bf(a), bf(HG_WIDTH),
                   jax.ShapeDtypeStruct((n, HG_WIDTH), F32), bf(HG_WIDTH), bf(HG_WIDTH)],
        compiler_params=pltpu.CompilerParams(
            dimension_semantics=("arbitrary",), vmem_limit_bytes=VMEM_LIMIT),
        name="in_proj",
    )(x2d, nw, w_in, seg, qw, kw, cos, slo, shi)


def _attn_block(q2, kk, vv, valid):
    w = ATTN_BLOCK
    lane = lax.broadcasted_iota(jnp.int32, (w, LANES), 1)
    head0 = lane < HEAD_DIM
    zero = jnp.zeros_like(q2)
    qq = jnp.concatenate([jnp.where(head0, q2, zero), jnp.where(head0, zero, q2)], axis=0)
    s = lax.dot_general(qq, kk, (((1,), (1,)), ((), ())), preferred_element_type=F32)
    s = jnp.where(valid, s, NEG_INF)
    m = jnp.max(s, axis=1, keepdims=True)
    p = jnp.exp(s - m)
    den = jnp.sum(p, axis=1, keepdims=True)
    pv = jnp.dot(p.astype(BF16), vv, preferred_element_type=F32)
    o = pv * (1.0 / den)
    lse = m + jnp.log(den)
    return jnp.where(head0, o[:w], o[w:]), lse


def _attn_kernel(q_ref, k_ref, v_ref, o_ref, lse_ref, *, groups, nblk):
    w = ATTN_BLOCK
    pairs = ATTN_WIDTH // LANES
    lane = lax.broadcasted_iota(jnp.int32, (w, LANES), 1)
    qi1 = lax.broadcasted_iota(jnp.int32, (2 * w, w), 0) % w
    kj1 = lax.broadcasted_iota(jnp.int32, (2 * w, w), 1)
    valid_first = kj1 <= qi1
    qi2 = lax.broadcasted_iota(jnp.int32, (2 * w, 2 * w), 0) % w
    kj2 = lax.broadcasted_iota(jnp.int32, (2 * w, 2 * w), 1)
    dist = qi2 + w - kj2
    valid_band = (dist >= 0) & (dist <= w)

    def do_block(g, qb, rows_q, rows_k, valid):
        acc = jnp.zeros((w, LANES), F32)
        for hp in range(pairs):
            c0 = g * ATTN_WIDTH + hp * LANES
            o, lse = _attn_block(q_ref[0, rows_q, c0:c0 + LANES],
                                 k_ref[0, rows_k, c0:c0 + LANES],
                                 v_ref[0, rows_k, c0:c0 + LANES], valid)
            o_ref[0, rows_q, c0:c0 + LANES] = o
            acc = jnp.where(lane == 2 * hp, lse[:w], acc)
            acc = jnp.where(lane == 2 * hp + 1, lse[w:], acc)
        lse_ref[0, g, qb] = acc.T[:ATTN_HEADS, :]

    for g in range(groups):
        do_block(g, 0, pl.ds(0, w), pl.ds(0, w), valid_first)
        if nblk > 1:
            def body(qb, carry, g=g):
                r0 = pl.multiple_of(qb * w, w)
                do_block(g, qb, pl.ds(r0, w), pl.ds(r0 - w, 2 * w), valid_band)
                return carry
            lax.fori_loop(1, nblk, body, 0)


def _attention(q, k, v, dilation, groups):
    b, s, a = q.shape
    l = s // dilation
    nblk = l // ATTN_BLOCK
    view = lambda t: t.reshape(b, l, dilation * a)
    spec = pl.BlockSpec((1, l, groups * a), lambda i, r: (i, 0, r))
    o, lse = pl.pallas_call(
        functools.partial(_attn_kernel, groups=groups, nblk=nblk),
        grid=(b, dilation // groups),
        in_specs=[spec, spec, spec],
        out_specs=[spec,
                   pl.BlockSpec((1, groups, nblk, ATTN_HEADS, ATTN_BLOCK),
                                lambda i, r: (i, r, 0, 0, 0))],
        out_shape=[jax.ShapeDtypeStruct((b, l, dilation * a), F32),
                   jax.ShapeDtypeStruct((b, dilation, nblk, ATTN_HEADS, ATTN_BLOCK), F32)],
        compiler_params=pltpu.CompilerParams(
            dimension_semantics=("arbitrary", "arbitrary"), vmem_limit_bytes=VMEM_LIMIT),
        name=f"attn_d{dilation}",
    )(view(q), view(k), view(v))
    lse = lse.transpose(0, 2, 4, 1, 3).reshape(b, s, ATTN_HEADS)
    return o.reshape(b, s, a), lse


def _hgrn_kernel(lbl_ref, q_ref, f_ref, i_ref, g_ref, nw_ref, o_ref,
                 qd_scr, intra_scr, cs_scr, dec_scr, *, seq, layer):
    c = HG_CHUNK
    t2 = 2 * c
    ntile = seq // t2
    lg = lbl_ref[...]
    e = jnp.exp(lg - jnp.max(lg, axis=0, keepdims=True))
    lb = jnp.sum(e[0:layer + 1, :], axis=0, keepdims=True) / jnp.sum(e, axis=0, keepdims=True)

    row = lax.broadcasted_iota(jnp.int32, (t2, HG_DIM), 0)
    rin = row % c
    col = lax.broadcasted_iota(jnp.int32, (t2, t2), 1)
    rr = lax.broadcasted_iota(jnp.int32, (t2, t2), 0)
    tril = ((rr // c) == (col // c)) & ((col % c) <= (rr % c))

    def phase_a(t, carry):
        r0 = pl.multiple_of(t * t2, t2)
        rows = pl.ds(r0, t2)
        f = lb + (1.0 - lb) * (1.0 / (1.0 + jnp.exp(-f_ref[0, rows, :])))
        gl = jnp.log(f)
        kk = 1.0 - f
        for sft in (1, 2, 4, 8, 16, 32):
            gl = gl + jnp.where(rin >= sft, pltpu.roll(gl, sft, axis=0), 0.0)
        g_last = jnp.where(row < c, gl[c - 1:c, :], gl[t2 - 1:t2, :])
        qd = q_ref[0, rows, :].astype(F32) * jnp.exp(gl)
        kd = (kk * jnp.exp(-gl)).astype(BF16)
        kl = (kk * jnp.exp(g_last - gl)).astype(BF16)
        qd16 = qd.astype(BF16)
        v = i_ref[0, rows, :]
        att = lax.dot_general(qd16, kd, (((1,), (1,)), ((), ())), preferred_element_type=F32)
        att = jnp.where(tril, att, 0.0).astype(BF16)
        intra_scr[rows, :] = jnp.dot(att, v, preferred_element_type=F32)
        qd_scr[rows, :] = qd16
        for j in range(2):
            sl = slice(j * c, (j + 1) * c)
            cs_scr[2 * t + j] = lax.dot_general(v[sl], kl[sl], (((0,), (0,)), ((), ())),
                                                preferred_element_type=F32)
            dec_scr[2 * t + j] = jnp.exp(gl[(j + 1) * c - 1:(j + 1) * c, :])
        return carry

    lax.fori_loop(0, ntile, phase_a, 0)

    nw = nw_ref[...]

    def phase_b(ch, state_t):
        r0 = pl.multiple_of(ch * c, c)
        rows = pl.ds(r0, c)
        inter = lax.dot_general(qd_scr[rows, :], state_t.astype(BF16),
                                (((1,), (1,)), ((), ())), preferred_element_type=F32)
        rec = intra_scr[rows, :] + inter
        gate = g_ref[0, rows, :].astype(F32)
        o_ref[0, rows, :] = (_rms_norm_rows(rec, nw) * _silu(gate)).astype(BF16)
        return state_t * dec_scr[ch] + cs_scr[ch]

    lax.fori_loop(0, seq // c, phase_b, jnp.zeros((HG_DIM, HG_DIM), F32))


def _hgrn2(lb_logits, hq, hf, hi, hg, nw, layer):
    b, s, _ = hq.shape
    nl = lb_logits.shape[0]
    nchunk = s // HG_CHUNK
    spec = pl.BlockSpec((1, s, HG_DIM), lambda i, h: (i, 0, h))
    return pl.pallas_call(
        functools.partial(_hgrn_kernel, seq=s, layer=layer),
        grid=(b, HG_HEADS),
        in_specs=[pl.BlockSpec((nl, HG_DIM), lambda i, h: (0, h)),
                  spec, spec, spec, spec,
                  pl.BlockSpec((1, HG_DIM), lambda i, h: (0, 0))],
        out_specs=spec,
        out_shape=jax.ShapeDtypeStruct((b, s, HG_WIDTH), BF16),
        scratch_shapes=[pltpu.VMEM((s, HG_DIM), BF16),
                        pltpu.VMEM((s, HG_DIM), F32),
                        pltpu.VMEM((nchunk, HG_DIM, HG_DIM), F32),
                        pltpu.VMEM((nchunk, 1, HG_DIM), F32)],
        compiler_params=pltpu.CompilerParams(
            dimension_semantics=("arbitrary", "arbitrary"), vmem_limit_bytes=VMEM_LIMIT),
        name="hgrn2",
    )(lb_logits, hq, hf, hi, hg, nw)


def _out_ffn2_kernel(x_ref, o1_ref, o2_ref, o3_ref, l1_ref, l2_ref, l3_ref, rec_ref,
                     wo_ref, nw_ref, w1_ref, w3_ref, w2_ref, out_ref):
    tm = x_ref.shape[0]
    o_refs = (o1_ref, o2_ref, o3_ref)
    ls = [r[...] for r in (l1_ref, l2_ref, l3_ref)]
    mx = jnp.maximum(jnp.maximum(ls[0], ls[1]), ls[2])
    es = [jnp.exp(l - mx) for l in ls]
    inv = 1.0 / (es[0] + es[1] + es[2])
    ws = [e * inv for e in es]
    lane = lax.broadcasted_iota(jnp.int32, (tm, LANES), 1)
    head0 = lane < HEAD_DIM
    parts = []
    for hp in range(ATTN_WIDTH // LANES):
        acc = jnp.zeros((tm, LANES), F32)
        for wgt, o_ref in zip(ws, o_refs):
            wl = jnp.where(head0, wgt[:, 2 * hp:2 * hp + 1], wgt[:, 2 * hp + 1:2 * hp + 2])
            acc = acc + wl * o_ref[:, hp * LANES:(hp + 1) * LANES]
        parts.append(acc.astype(BF16))
    mixed = jnp.concatenate(parts + [rec_ref[...]], axis=1)
    x2 = x_ref[...] + jnp.dot(mixed, wo_ref[...], preferred_element_type=F32)
    out_ref[...] = _swiglu_half_step(x2, nw_ref[...], w1_ref, w3_ref, w2_ref)


def _out_ffn2(x2d, os_, lses, rec, wo, nw, w1, w3, w2, tm):
    n, d = x2d.shape
    f = w1.shape[1]
    a = ATTN_WIDTH
    row = lambda i: (i, 0)
    return pl.pallas_call(
        _out_ffn2_kernel,
        grid=(n // tm,),
        in_specs=[pl.BlockSpec((tm, d), row)]
                 + [pl.BlockSpec((tm, a), row)] * 3
                 + [pl.BlockSpec((tm, ATTN_HEADS), row)] * 3
                 + [pl.BlockSpec((tm, HG_WIDTH), row),
                    _const_spec((a + HG_WIDTH, d)),
                    _const_spec((1, d)),
                    _const_spec((d, f)),
                    _const_spec((d, f)),
                    _const_spec((f, d))],
        out_specs=pl.BlockSpec((tm, d), row),
        out_shape=jax.ShapeDtypeStruct((n, d), F32),
        compiler_params=pltpu.CompilerParams(
            dimension_semantics=("arbitrary",), vmem_limit_bytes=VMEM_LIMIT),
        name="out_ffn2",
    )(x2d, *os_, *lses, rec, wo, nw, w1, w3, w2)


def _rope_lane_tables(s):
    half = ROPE_DIM // 2
    inv = ROPE_THETA ** (-jnp.arange(0, ROPE_DIM, 2, dtype=F32) / ROPE_DIM)
    ang = jnp.arange(s, dtype=F32)[:, None] * inv[None, :]
    cos, sin = jnp.cos(ang), jnp.sin(ang)
    dim = jnp.arange(LANES) % HEAD_DIM
    idx = dim % half
    c = jnp.where(dim[None, :] < ROPE_DIM, cos[:, idx], 1.0)
    s_lo = jnp.where(dim[None, :] < half, -sin[:, idx], 0.0)
    s_hi = jnp.where((dim[None, :] >= half) & (dim[None, :] < ROPE_DIM), sin[:, idx], 0.0)
    return c.astype(F32), s_lo.astype(F32), s_hi.astype(F32)


def kernel(x, ffn1_norm, ffn1_w1, ffn1_w3, ffn1_w2, mix_norm, w_in, q_norm, k_norm,
           hg_lb_logits, hg_out_norm, w_out, ffn2_norm, ffn2_w1, ffn2_w3, ffn2_w2):
    b, s, d = x.shape
    depth = ffn1_norm.shape[0]
    n = b * s
    tm = 512
    cos, s_lo, s_hi = _rope_lane_tables(s)
    head_of = jnp.arange(ATTN_WIDTH) // HEAD_DIM
    seg = (head_of[:, None] == head_of[None, :]).astype(BF16)
    x2d = x.reshape(n, d)
    for layer in range(depth):
        x1 = _ffn1(x2d, ffn1_norm[layer][None, :], ffn1_w1[layer].astype(BF16),
                   ffn1_w3[layer].astype(BF16), ffn1_w2[layer].astype(BF16), tm)
        q, k, v, hq, hf, hi, hg = _in_proj(
            x1, mix_norm[layer][None, :], w_in[layer].astype(BF16), seg,
            jnp.tile(q_norm[layer], ATTN_HEADS)[None, :], jnp.tile(k_norm[layer], ATTN_HEADS)[None, :],
            cos, s_lo, s_hi, tm, s)
        q3, k3, v3 = (t.reshape(b, s, ATTN_WIDTH) for t in (q, k, v))
        os_, lses = [], []
        for (_, dilation), groups in zip(DILATED_PATTERNS, (1, 1, 4)):
            o, lse = _attention(q3, k3, v3, dilation, groups)
            os_.append(o.reshape(n, ATTN_WIDTH))
            lses.append(lse.reshape(n, ATTN_HEADS))
        rec = _hgrn2(hg_lb_logits, hq.reshape(b, s, HG_WIDTH), hf.reshape(b, s, HG_WIDTH),
                     hi.reshape(b, s, HG_WIDTH), hg.reshape(b, s, HG_WIDTH),
                     hg_out_norm[layer][None, :], layer)
        x2d = _out_ffn2(x1, os_, lses, rec.reshape(n, HG_WIDTH), w_out[layer].astype(BF16),
                        ffn2_norm[layer][None, :], ffn2_w1[layer].astype(BF16),
                        ffn2_w3[layer].astype(BF16), ffn2_w2[layer].astype(BF16), tm)
    return x2d.reshape(b, s, d)
```

```python
import functools

import jax
import jax.numpy as jnp
from jax import lax
from jax.experimental import pallas as pl
from jax.experimental.pallas import tpu as pltpu

F32 = jnp.float32
BF16 = jnp.bfloat16

EPS = 1e-6
NEG_INF = -1e30
HEAD_DIM = 64
ATTN_HEADS = 8
ATTN_WIDTH = ATTN_HEADS * HEAD_DIM
ROPE_DIM = HEAD_DIM // 4
ROPE_THETA = 500000.0
DILATIONS = (1, 4, 16)
HG_HEADS = 4
HG_DIM = 128
HG_CHUNK = 64
HG_WIDTH = HG_HEADS * HG_DIM

LANES = 128
ATTN_BLOCK = 128
PAIRS = ATTN_WIDTH // LANES
VMEM_LIMIT = 56 * 1024 * 1024


def _const_spec(shape):
    nd = len(shape)
    return pl.BlockSpec(shape, lambda *_: (0,) * nd, pipeline_mode=pl.Buffered(1))


def _rms_norm_rows(x, w):
    return x * lax.rsqrt(jnp.mean(x * x, axis=-1, keepdims=True) + EPS) * w


def _silu(a):
    return a * (1.0 / (1.0 + jnp.exp(-a)))


def _swiglu_half_step(x, nw, w1_ref, w3_ref, w2_ref):
    h = _rms_norm_rows(x, nw).astype(BF16)
    a = jnp.dot(h, w1_ref[...], preferred_element_type=F32)
    b = jnp.dot(h, w3_ref[...], preferred_element_type=F32)
    g = (_silu(a) * b).astype(BF16)
    y = jnp.dot(g, w2_ref[...], preferred_element_type=F32)
    return x + 0.5 * y


def _split_bf16(x):
    hi = x.astype(BF16)
    return hi, (x - hi.astype(F32)).astype(BF16)


def _ffn1_kernel(x_ref, nw_ref, w1_ref, w3_ref, w2_ref, o_ref):
    o_ref[...] = _swiglu_half_step(x_ref[...], nw_ref[...], w1_ref, w3_ref, w2_ref)


def _ffn1(x2d, nw, w1, w3, w2, tm):
    n, d = x2d.shape
    f = w1.shape[1]
    return pl.pallas_call(
        _ffn1_kernel,
        grid=(n // tm,),
        in_specs=[
            pl.BlockSpec((tm, d), lambda i: (i, 0)),
            _const_spec((1, d)),
            _const_spec((d, f)),
            _const_spec((d, f)),
            _const_spec((f, d)),
        ],
        out_specs=pl.BlockSpec((tm, d), lambda i: (i, 0)),
        out_shape=jax.ShapeDtypeStruct((n, d), F32),
        compiler_params=pltpu.CompilerParams(
            dimension_semantics=("arbitrary",), vmem_limit_bytes=VMEM_LIMIT),
        name="ffn1",
    )(x2d, nw, w1, w3, w2)


def _head_norm_rope(t, seg_ref, w, cos, sin_lo, sin_hi, scale):
    hi, lo = _split_bf16(t * t)
    seg = seg_ref[...]
    ms = (jnp.dot(hi, seg, preferred_element_type=F32)
          + jnp.dot(lo, seg, preferred_element_type=F32)) * (1.0 / HEAD_DIM)
    y = t * lax.rsqrt(ms + EPS) * w
    outs = []
    for c in range(PAIRS):
        yc = y[:, c * LANES:(c + 1) * LANES]
        from_hi = pltpu.roll(yc, LANES - ROPE_DIM // 2, axis=1)
        from_lo = pltpu.roll(yc, ROPE_DIM // 2, axis=1)
        outs.append((yc * cos + from_hi * sin_lo + from_lo * sin_hi) * scale)
    return outs


def _in_proj_kernel(x_ref, nw_ref, w_ref, seg_ref, qw_ref, kw_ref, cos_ref, slo_ref, shi_ref,
                    q1_ref, k1_ref, v1_ref, q4_ref, k4_ref, v4_ref, q16_ref, k16_ref, v16_ref,
                    hq_ref, hf_ref, hi_ref, hg_ref, slab_scr):
    tm = x_ref.shape[0]
    h = _rms_norm_rows(x_ref[...], nw_ref[...]).astype(BF16)
    p = jnp.dot(h, w_ref[...], preferred_element_type=F32)
    a = ATTN_WIDTH
    cos, slo, shi = cos_ref[...], slo_ref[...], shi_ref[...]
    qs = _head_norm_rope(p[:, 0:a], seg_ref, qw_ref[...], cos, slo, shi, HEAD_DIM ** -0.5)
    ks = _head_norm_rope(p[:, a:2 * a], seg_ref, kw_ref[...], cos, slo, shi, 1.0)
    vs = [p[:, 2 * a + c * LANES:2 * a + (c + 1) * LANES] for c in range(PAIRS)]
    outs = ((q1_ref, q4_ref, q16_ref), (k1_ref, k4_ref, k16_ref), (v1_ref, v4_ref, v16_ref))
    for ti, slabs in enumerate((qs, ks, vs)):
        o1, o4, o16 = outs[ti]
        for c in range(PAIRS):
            lanes = slice(c * LANES, (c + 1) * LANES)
            o1[0, :, lanes] = slabs[c].astype(BF16)
            slab_scr[ti * PAIRS + c] = slabs[c]
            for dil, oref in ((4, o4), (16, o16)):
                for r in range(dil):
                    rows = slab_scr[ti * PAIRS + c, pl.ds(r, tm // dil, stride=dil), :]
                    oref[0, r, :, lanes] = rows.astype(BF16)
    o = 3 * a
    hq_ref[0] = p[:, o:o + HG_WIDTH].astype(BF16)
    hf_ref[0] = p[:, o + HG_WIDTH:o + 2 * HG_WIDTH]
    hi_ref[0] = p[:, o + 2 * HG_WIDTH:o + 3 * HG_WIDTH].astype(BF16)
    hg_ref[0] = p[:, o + 3 * HG_WIDTH:o + 4 * HG_WIDTH].astype(BF16)


def _in_proj(x3d, nw, w_in, seg, qw, kw, cos, slo, shi, tm):
    b, s, d = x3d.shape
    cols = w_in.shape[1]
    a = ATTN_WIDTH
    row = lambda i, j: (i, j, 0)
    pos = lambda i, j: (j, 0)
    out_specs, out_shape = [], []
    for dil in DILATIONS:
        for _ in range(3):
            if dil == 1:
                out_specs.append(pl.BlockSpec((1, tm, a), row))
                out_shape.append(jax.ShapeDtypeStruct((b, s, a), BF16))
            else:
                out_specs.append(pl.BlockSpec((1, dil, tm // dil, a), lambda i, j: (i, 0, j, 0)))
                out_shape.append(jax.ShapeDtypeStruct((b, dil, s // dil, a), BF16))
    for dt in (BF16, F32, BF16, BF16):
        out_specs.append(pl.BlockSpec((1, tm, HG_WIDTH), row))
        out_shape.append(jax.ShapeDtypeStruct((b, s, HG_WIDTH), dt))
    return pl.pallas_call(
        _in_proj_kernel,
        grid=(b, s // tm),
        in_specs=[
            pl.BlockSpec((None, tm, d), row),
            _const_spec((1, d)),
            _const_spec((d, cols)),
            _const_spec((a, a)),
            _const_spec((1, a)),
            _const_spec((1, a)),
            pl.BlockSpec((tm, LANES), pos),
            pl.BlockSpec((tm, LANES), pos),
            pl.BlockSpec((tm, LANES), pos),
        ],
        out_specs=out_specs,
        out_shape=out_shape,
        scratch_shapes=[pltpu.VMEM((3 * PAIRS, tm, LANES), F32)],
        compiler_params=pltpu.CompilerParams(
            dimension_semantics=("arbitrary", "arbitrary"), vmem_limit_bytes=VMEM_LIMIT),
        name="in_proj",
    )(x3d, nw, w_in, seg, qw, kw, cos, slo, shi)


def _attn_block(q2, kk, vv, valid):
    w = ATTN_BLOCK
    lane = lax.broadcasted_iota(jnp.int32, (w, LANES), 1)
    head0 = lane < HEAD_DIM
    zero = jnp.zeros_like(q2)
    qq = jnp.concatenate([jnp.where(head0, q2, zero), jnp.where(head0, zero, q2)], axis=0)
    s = lax.dot_general(qq, kk, (((1,), (1,)), ((), ())), preferred_element_type=F32)
    s = jnp.where(valid, s, NEG_INF)
    m = jnp.max(s, axis=1, keepdims=True)
    p = jnp.exp(s - m)
    den = jnp.sum(p, axis=1, keepdims=True)
    pv = jnp.dot(p.astype(BF16), vv, preferred_element_type=F32)
    return jnp.where(head0, pv[:w], pv[w:]), m, den


def _attn_kernel(q_ref, k_ref, v_ref, o_ref, m_ref, den_ref, *, dilation, groups, nblk):
    w = ATTN_BLOCK
    lane = lax.broadcasted_iota(jnp.int32, (w, LANES), 1)
    qi1 = lax.broadcasted_iota(jnp.int32, (2 * w, w), 0) % w
    kj1 = lax.broadcasted_iota(jnp.int32, (2 * w, w), 1)
    valid_first = kj1 <= qi1
    qi2 = lax.broadcasted_iota(jnp.int32, (2 * w, 2 * w), 0) % w
    kj2 = lax.broadcasted_iota(jnp.int32, (2 * w, 2 * w), 1)
    dist = qi2 + w - kj2
    valid_band = (dist >= 0) & (dist <= w)
    res0 = pl.program_id(1) * groups

    def do_block(g, qb, r0, rows_k, valid):
        rows_q = pl.ds(r0, w)
        m_acc = jnp.zeros((w, LANES), F32)
        d_acc = jnp.zeros((w, LANES), F32)
        for hp in range(PAIRS):
            lanes = slice(hp * LANES, (hp + 1) * LANES)
            o, m, den = _attn_block(q_ref[0, g, rows_q, lanes], k_ref[0, g, rows_k, lanes],
                                    v_ref[0, g, rows_k, lanes], valid)
            if dilation == 1:
                o_ref[0, hp, rows_q, :] = o
            else:
                tok0 = r0 * dilation + res0 + g
                o_ref[0, hp, pl.ds(tok0, w, stride=dilation), :] = o
            for h in range(2):
                sel = lane == 2 * hp + h
                m_acc = jnp.where(sel, m[h * w:(h + 1) * w], m_acc)
                d_acc = jnp.where(sel, den[h * w:(h + 1) * w], d_acc)
        m_ref[0, g, qb] = m_acc.T[:ATTN_HEADS, :]
        den_ref[0, g, qb] = d_acc.T[:ATTN_HEADS, :]

    for g in range(groups):
        do_block(g, 0, 0, pl.ds(0, w), valid_first)
        if nblk > 1:
            def body(qb, carry, g=g):
                r0 = pl.multiple_of(qb * w, w)
                do_block(g, qb, r0, pl.ds(r0 - w, 2 * w), valid_band)
                return carry
            lax.fori_loop(1, nblk, body, 0)


def _attention(q, k, v, dilation, groups):
    b, _, l, a = q.shape
    s = l * dilation
    nblk = l // ATTN_BLOCK
    spec = pl.BlockSpec((1, groups, l, a), lambda i, r: (i, r, 0, 0))
    stat_spec = pl.BlockSpec((1, groups, nblk, ATTN_HEADS, ATTN_BLOCK), lambda i, r: (i, r, 0, 0, 0))
    stat_shape = jax.ShapeDtypeStruct((b, dilation, nblk, ATTN_HEADS, ATTN_BLOCK), F32)
    o, m, den = pl.pallas_call(
        functools.partial(_attn_kernel, dilation=dilation, groups=groups, nblk=nblk),
        grid=(b, dilation // groups),
        in_specs=[spec, spec, spec],
        out_specs=[pl.BlockSpec((1, PAIRS, s, LANES), lambda i, r: (i, 0, 0, 0)),
                   stat_spec, stat_spec],
        out_shape=[jax.ShapeDtypeStruct((b, PAIRS, s, LANES), F32), stat_shape, stat_shape],
        compiler_params=pltpu.CompilerParams(
            dimension_semantics=("arbitrary", "arbitrary"), vmem_limit_bytes=VMEM_LIMIT),
        name=f"attn_d{dilation}",
    )(q, k, v)
    nat = lambda t: t.transpose(0, 2, 4, 1, 3).reshape(b, s, ATTN_HEADS)
    return o, nat(m), nat(den)


def _hgrn_kernel(lbl_ref, q_ref, f_ref, i_ref, g_ref, nw_ref, o_ref,
                 qd_scr, intra_scr, cs_scr, dec_scr, st_scr, *, seq, layer):
    c = HG_CHUNK
    t2 = 2 * c
    ntile = seq // t2
    lg = lbl_ref[...]
    e = jnp.exp(lg - jnp.max(lg, axis=0, keepdims=True))
    lb = jnp.sum(e[0:layer + 1, :], axis=0, keepdims=True) / jnp.sum(e, axis=0, keepdims=True)

    row = lax.broadcasted_iota(jnp.int32, (t2, HG_DIM), 0)
    rin = row % c
    col = lax.broadcasted_iota(jnp.int32, (t2, t2), 1)
    rr = lax.broadcasted_iota(jnp.int32, (t2, t2), 0)
    tril = ((rr // c) == (col // c)) & ((col % c) <= (rr % c))

    def intra_step(t, carry):
        r0 = pl.multiple_of(t * t2, t2)
        rows = pl.ds(r0, t2)
        f = lb + (1.0 - lb) * (1.0 / (1.0 + jnp.exp(-f_ref[0, rows, :])))
        gl = jnp.log(f)
        kk = 1.0 - f
        for sft in (1, 2, 4, 8, 16, 32):
            gl = gl + jnp.where(rin >= sft, pltpu.roll(gl, sft, axis=0), 0.0)
        g_last = jnp.where(row < c, gl[c - 1:c, :], gl[t2 - 1:t2, :])
        qd = q_ref[0, rows, :].astype(F32) * jnp.exp(gl)
        kd = (kk * jnp.exp(-gl)).astype(BF16)
        kl = (kk * jnp.exp(g_last - gl)).astype(BF16)
        qd16 = qd.astype(BF16)
        v = i_ref[0, rows, :]
        att = lax.dot_general(qd16, kd, (((1,), (1,)), ((), ())), preferred_element_type=F32)
        att = jnp.where(tril, att, 0.0).astype(BF16)
        intra_scr[rows, :] = jnp.dot(att, v, preferred_element_type=F32)
        qd_scr[rows, :] = qd16
        for j in range(2):
            sl = slice(j * c, (j + 1) * c)
            cs_scr[2 * t + j] = lax.dot_general(v[sl], kl[sl], (((0,), (0,)), ((), ())),
                                                preferred_element_type=F32)
            dec_scr[2 * t + j] = jnp.exp(gl[(j + 1) * c - 1:(j + 1) * c, :])
        return carry

    lax.fori_loop(0, ntile, intra_step, 0, unroll=2)

    def scan_step(ch, state_t):
        st_scr[ch] = state_t.astype(BF16)
        return state_t * dec_scr[ch] + cs_scr[ch]

    lax.fori_loop(0, seq // c, scan_step, jnp.zeros((HG_DIM, HG_DIM), F32), unroll=4)

    nw = nw_ref[...]

    def out_step(t, carry):
        r0 = pl.multiple_of(t * t2, t2)
        rows = pl.ds(r0, t2)
        inter = jnp.concatenate(
            [lax.dot_general(qd_scr[pl.ds(r0 + j * c, c), :], st_scr[2 * t + j],
                             (((1,), (1,)), ((), ())), preferred_element_type=F32)
             for j in range(2)], axis=0)
        rec = intra_scr[rows, :] + inter
        gate = g_ref[0, rows, :].astype(F32)
        o_ref[0, rows, :] = (_rms_norm_rows(rec, nw) * _silu(gate)).astype(BF16)
        return carry

    lax.fori_loop(0, ntile, out_step, 0, unroll=2)


def _hgrn2(lb_logits, hq, hf, hi, hg, nw, layer):
    b, s, _ = hq.shape
    nl = lb_logits.shape[0]
    nchunk = s // HG_CHUNK
    spec = pl.BlockSpec((1, s, HG_DIM), lambda i, h: (i, 0, h))
    return pl.pallas_call(
        functools.partial(_hgrn_kernel, seq=s, layer=layer),
        grid=(b, HG_HEADS),
        in_specs=[pl.BlockSpec((nl, HG_DIM), lambda i, h: (0, h)),
                  spec, spec, spec, spec,
                  pl.BlockSpec((1, HG_DIM), lambda i, h: (0, 0))],
        out_specs=spec,
        out_shape=jax.ShapeDtypeStruct((b, s, HG_WIDTH), BF16),
        scratch_shapes=[pltpu.VMEM((s, HG_DIM), BF16),
                        pltpu.VMEM((s, HG_DIM), F32),
                        pltpu.VMEM((nchunk, HG_DIM, HG_DIM), F32),
                        pltpu.VMEM((nchunk, 1, HG_DIM), F32),
                        pltpu.VMEM((nchunk, HG_DIM, HG_DIM), BF16)],
        compiler_params=pltpu.CompilerParams(
            dimension_semantics=("arbitrary", "arbitrary"), vmem_limit_bytes=VMEM_LIMIT),
        name="hgrn2",
    )(lb_logits, hq, hf, hi, hg, nw)


def _out_ffn2_kernel(x_ref, o1_ref, o2_ref, o3_ref, m1_ref, m2_ref, m3_ref,
                     d1_ref, d2_ref, d3_ref, rec_ref,
                     wo_ref, nw_ref, w1_ref, w3_ref, w2_ref, out_ref):
    o_refs = (o1_ref, o2_ref, o3_ref)
    ms = [r[0] for r in (m1_ref, m2_ref, m3_ref)]
    dens = [r[0] for r in (d1_ref, d2_ref, d3_ref)]
    mx = jnp.maximum(jnp.maximum(ms[0], ms[1]), ms[2])
    es = [jnp.exp(m - mx) for m in ms]
    inv = 1.0 / (es[0] * dens[0] + es[1] * dens[1] + es[2] * dens[2])
    head = lax.broadcasted_iota(jnp.int32, (ATTN_HEADS, ATTN_WIDTH), 0)
    lane_head = lax.broadcasted_iota(jnp.int32, (ATTN_HEADS, ATTN_WIDTH), 1) // HEAD_DIM
    expand = (head == lane_head).astype(BF16)
    wide = []
    for e in es:
        hi, lo = _split_bf16(e * inv)
        wide.append(jnp.dot(hi, expand, preferred_element_type=F32)
                    + jnp.dot(lo, expand, preferred_element_type=F32))
    parts = []
    for hp in range(PAIRS):
        lanes = slice(hp * LANES, (hp + 1) * LANES)
        acc = wide[0][:, lanes] * o_refs[0][0, hp]
        for p in (1, 2):
            acc = acc + wide[p][:, lanes] * o_refs[p][0, hp]
        parts.append(acc.astype(BF16))
    mixed = jnp.concatenate(parts + [rec_ref[0]], axis=1)
    x2 = x_ref[0] + jnp.dot(mixed, wo_ref[...], preferred_element_type=F32)
    out_ref[0] = _swiglu_half_step(x2, nw_ref[...], w1_ref, w3_ref, w2_ref)


def _out_ffn2(x3d, os_, ms, dens, rec, wo, nw, w1, w3, w2, tm):
    b, s, d = x3d.shape
    f = w1.shape[1]
    a = ATTN_WIDTH
    row = lambda i, j: (i, j, 0)
    return pl.pallas_call(
        _out_ffn2_kernel,
        grid=(b, s // tm),
        in_specs=[pl.BlockSpec((1, tm, d), row)]
                 + [pl.BlockSpec((1, PAIRS, tm, LANES), lambda i, j: (i, 0, j, 0))] * 3
                 + [pl.BlockSpec((1, tm, ATTN_HEADS), row)] * 6
                 + [pl.BlockSpec((1, tm, HG_WIDTH), row),
                    _const_spec((a + HG_WIDTH, d)),
                    _const_spec((1, d)),
                    _const_spec((d, f)),
                    _const_spec((d, f)),
                    _const_spec((f, d))],
        out_specs=pl.BlockSpec((1, tm, d), row),
        out_shape=jax.ShapeDtypeStruct((b, s, d), F32),
        compiler_params=pltpu.CompilerParams(
            dimension_semantics=("arbitrary", "arbitrary"), vmem_limit_bytes=VMEM_LIMIT),
        name="out_ffn2",
    )(x3d, *os_, *ms, *dens, rec, wo, nw, w1, w3, w2)


def _rope_lane_tables(s):
    half = ROPE_DIM // 2
    inv = ROPE_THETA ** (-jnp.arange(0, ROPE_DIM, 2, dtype=F32) / ROPE_DIM)
    ang = jnp.arange(s, dtype=F32)[:, None] * inv[None, :]
    cos, sin = jnp.cos(ang), jnp.sin(ang)
    dim = jnp.arange(LANES) % HEAD_DIM
    idx = dim % half
    c = jnp.where(dim[None, :] < ROPE_DIM, cos[:, idx], 1.0)
    s_lo = jnp.where(dim[None, :] < half, -sin[:, idx], 0.0)
    s_hi = jnp.where((dim[None, :] >= half) & (dim[None, :] < ROPE_DIM), sin[:, idx], 0.0)
    return c.astype(F32), s_lo.astype(F32), s_hi.astype(F32)


def kernel(x, ffn1_norm, ffn1_w1, ffn1_w3, ffn1_w2, mix_norm, w_in, q_norm, k_norm,
           hg_lb_logits, hg_out_norm, w_out, ffn2_norm, ffn2_w1, ffn2_w3, ffn2_w2):
    b, s, d = x.shape
    depth = ffn1_norm.shape[0]
    tm = 512
    cos, s_lo, s_hi = _rope_lane_tables(s)
    head_of = jnp.arange(ATTN_WIDTH) // HEAD_DIM
    seg = (head_of[:, None] == head_of[None, :]).astype(BF16)
    for layer in range(depth):
        x1 = _ffn1(x.reshape(b * s, d), ffn1_norm[layer][None, :], ffn1_w1[layer].astype(BF16),
                   ffn1_w3[layer].astype(BF16), ffn1_w2[layer].astype(BF16), tm).reshape(b, s, d)
        (q1, k1, v1, q4, k4, v4, q16, k16, v16, hq, hf, hi, hg) = _in_proj(
            x1, mix_norm[layer][None, :], w_in[layer].astype(BF16), seg,
            jnp.tile(q_norm[layer], ATTN_HEADS)[None, :], jnp.tile(k_norm[layer], ATTN_HEADS)[None, :],
            cos, s_lo, s_hi, tm)
        os_, ms, dens = [], [], []
        for dilation, groups, (q, k, v) in zip(
                DILATIONS, (1, 1, 4),
                ((q1[:, None], k1[:, None], v1[:, None]), (q4, k4, v4), (q16, k16, v16))):
            o, m, den = _attention(q, k, v, dilation, groups)
            os_.append(o)
            ms.append(m)
            dens.append(den)
        rec = _hgrn2(hg_lb_logits, hq, hf, hi, hg, hg_out_norm[layer][None, :], layer)
        x = _out_ffn2(x1, os_, ms, dens, rec, w_out[layer].astype(BF16),
                      ffn2_norm[layer][None, :], ffn2_w1[layer].astype(BF16),
                      ffn2_w3[layer].astype(BF16), ffn2_w2[layer].astype(BF16), tm)
    return x
```

```python
import functools

import jax
import jax.numpy as jnp
from jax import lax
from jax.experimental import pallas as pl
from jax.experimental.pallas import tpu as pltpu

F32 = jnp.float32
BF16 = jnp.bfloat16

EPS = 1e-6
NEG_INF = -1e30
HEAD_DIM = 64
ATTN_HEADS = 8
ATTN_WIDTH = ATTN_HEADS * HEAD_DIM
ROPE_DIM = HEAD_DIM // 4
ROPE_THETA = 500000.0
DILATIONS = (1, 4, 16)
HG_HEADS = 4
HG_DIM = 128
HG_CHUNK = 64
HG_WIDTH = HG_HEADS * HG_DIM

LANES = 128
ATTN_BLOCK = 128
PAIRS = ATTN_WIDTH // LANES
VMEM_LIMIT = 56 * 1024 * 1024


def _const_spec(shape):
    nd = len(shape)
    return pl.BlockSpec(shape, lambda *_: (0,) * nd, pipeline_mode=pl.Buffered(1))


def _rms_norm_rows(x, w):
    return x * lax.rsqrt(jnp.mean(x * x, axis=-1, keepdims=True) + EPS) * w


def _silu(a):
    return a * (1.0 / (1.0 + jnp.exp(-a)))


def _swiglu_half_step(x, nw, w1_ref, w3_ref, w2_ref):
    h = _rms_norm_rows(x, nw).astype(BF16)
    a = jnp.dot(h, w1_ref[...], preferred_element_type=F32)
    b = jnp.dot(h, w3_ref[...], preferred_element_type=F32)
    g = (_silu(a) * b).astype(BF16)
    y = jnp.dot(g, w2_ref[...], preferred_element_type=F32)
    return x + 0.5 * y


def _split_bf16(x):
    hi = x.astype(BF16)
    return hi, (x - hi.astype(F32)).astype(BF16)


def _ffn1_kernel(x_ref, nw_ref, w1_ref, w3_ref, w2_ref, o_ref):
    o_ref[...] = _swiglu_half_step(x_ref[...], nw_ref[...], w1_ref, w3_ref, w2_ref)


def _ffn1(x2d, nw, w1, w3, w2, tm):
    n, d = x2d.shape
    f = w1.shape[1]
    return pl.pallas_call(
        _ffn1_kernel,
        grid=(n // tm,),
        in_specs=[
            pl.BlockSpec((tm, d), lambda i: (i, 0)),
            _const_spec((1, d)),
            _const_spec((d, f)),
            _const_spec((d, f)),
            _const_spec((f, d)),
        ],
        out_specs=pl.BlockSpec((tm, d), lambda i: (i, 0)),
        out_shape=jax.ShapeDtypeStruct((n, d), F32),
        compiler_params=pltpu.CompilerParams(
            dimension_semantics=("arbitrary",), vmem_limit_bytes=VMEM_LIMIT),
        name="ffn1",
    )(x2d, nw, w1, w3, w2)


def _head_norm_rope(t, seg_ref, w, cos, sin_lo, sin_hi, scale):
    hi, lo = _split_bf16(t * t)
    seg = seg_ref[...]
    sw = seg.shape[0]
    ms = jnp.concatenate(
        [jnp.dot(hi[:, c:c + sw], seg, preferred_element_type=F32)
         + jnp.dot(lo[:, c:c + sw], seg, preferred_element_type=F32)
         for c in range(0, ATTN_WIDTH, sw)], axis=1) * (1.0 / HEAD_DIM)
    y = t * lax.rsqrt(ms + EPS) * w
    outs = []
    for c in range(PAIRS):
        yc = y[:, c * LANES:(c + 1) * LANES]
        from_hi = pltpu.roll(yc, LANES - ROPE_DIM // 2, axis=1)
        from_lo = pltpu.roll(yc, ROPE_DIM // 2, axis=1)
        outs.append((yc * cos + from_hi * sin_lo + from_lo * sin_hi) * scale)
    return outs


def _in_proj_kernel(x_ref, nw_ref, w_ref, seg_ref, qw_ref, kw_ref, cos_ref, slo_ref, shi_ref,
                    q1_ref, k1_ref, v1_ref, q4_ref, k4_ref, v4_ref, q16_ref, k16_ref, v16_ref,
                    hq_ref, hf_ref, hi_ref, hg_ref, slab_scr, res4_scr):
    tm = x_ref.shape[0]
    h = _rms_norm_rows(x_ref[...], nw_ref[...]).astype(BF16)
    p = jnp.dot(h, w_ref[...], preferred_element_type=F32)
    a = ATTN_WIDTH
    cos, slo, shi = cos_ref[...], slo_ref[...], shi_ref[...]
    qs = _head_norm_rope(p[:, 0:a], seg_ref, qw_ref[...], cos, slo, shi, HEAD_DIM ** -0.5)
    ks = _head_norm_rope(p[:, a:2 * a], seg_ref, kw_ref[...], cos, slo, shi, 1.0)
    vs = [p[:, 2 * a + c * LANES:2 * a + (c + 1) * LANES] for c in range(PAIRS)]
    outs = ((q1_ref, q4_ref, q16_ref), (k1_ref, k4_ref, k16_ref), (v1_ref, v4_ref, v16_ref))
    for ti, slabs in enumerate((qs, ks, vs)):
        o1, o4, o16 = outs[ti]
        for c in range(PAIRS):
            lanes = slice(c * LANES, (c + 1) * LANES)
            o1[0, :, lanes] = slabs[c].astype(BF16)
            sl = ti * PAIRS + c
            slab_scr[sl] = slabs[c]
            for r in range(4):
                rows4 = slab_scr[sl, pl.ds(r, tm // 4, stride=4), :]
                o4[0, r, :, lanes] = rows4.astype(BF16)
                res4_scr[sl, r] = rows4
                for a_ in range(4):
                    rows16 = res4_scr[sl, r, pl.ds(a_, tm // 16, stride=4), :]
                    o16[0, 4 * a_ + r, :, lanes] = rows16.astype(BF16)
    o = 3 * a
    hq_ref[0] = p[:, o:o + HG_WIDTH].astype(BF16)
    hf_ref[0] = p[:, o + HG_WIDTH:o + 2 * HG_WIDTH]
    hi_ref[0] = p[:, o + 2 * HG_WIDTH:o + 3 * HG_WIDTH].astype(BF16)
    hg_ref[0] = p[:, o + 3 * HG_WIDTH:o + 4 * HG_WIDTH].astype(BF16)


def _in_proj(x3d, nw, w_in, seg, qw, kw, cos, slo, shi, tm):
    b, s, d = x3d.shape
    cols = w_in.shape[1]
    a = ATTN_WIDTH
    row = lambda i, j: (i, j, 0)
    pos = lambda i, j: (j, 0)
    out_specs, out_shape = [], []
    for dil in DILATIONS:
        for _ in range(3):
            if dil == 1:
                out_specs.append(pl.BlockSpec((1, tm, a), row))
                out_shape.append(jax.ShapeDtypeStruct((b, s, a), BF16))
            else:
                out_specs.append(pl.BlockSpec((1, dil, tm // dil, a), lambda i, j: (i, 0, j, 0)))
                out_shape.append(jax.ShapeDtypeStruct((b, dil, s // dil, a), BF16))
    for dt in (BF16, F32, BF16, BF16):
        out_specs.append(pl.BlockSpec((1, tm, HG_WIDTH), row))
        out_shape.append(jax.ShapeDtypeStruct((b, s, HG_WIDTH), dt))
    return pl.pallas_call(
        _in_proj_kernel,
        grid=(b, s // tm),
        in_specs=[
            pl.BlockSpec((None, tm, d), row),
            _const_spec((1, d)),
            _const_spec((d, cols)),
            _const_spec(seg.shape),
            _const_spec((1, a)),
            _const_spec((1, a)),
            pl.BlockSpec((tm, LANES), pos),
            pl.BlockSpec((tm, LANES), pos),
            pl.BlockSpec((tm, LANES), pos),
        ],
        out_specs=out_specs,
        out_shape=out_shape,
        scratch_shapes=[pltpu.VMEM((3 * PAIRS, tm, LANES), F32),
                        pltpu.VMEM((3 * PAIRS, 4, tm // 4, LANES), F32)],
        compiler_params=pltpu.CompilerParams(
            dimension_semantics=("arbitrary", "arbitrary"), vmem_limit_bytes=VMEM_LIMIT),
        name="in_proj",
    )(x3d, nw, w_in, seg, qw, kw, cos, slo, shi)


def _attn_block(q2, kk, vv, valid):
    w = ATTN_BLOCK
    lane = lax.broadcasted_iota(jnp.int32, (w, LANES), 1)
    head0 = lane < HEAD_DIM
    zero = jnp.zeros_like(q2)
    qq = jnp.concatenate([jnp.where(head0, q2, zero), jnp.where(head0, zero, q2)], axis=0)
    s = lax.dot_general(qq, kk, (((1,), (1,)), ((), ())), preferred_element_type=F32)
    s = jnp.where(valid, s, NEG_INF)
    m = jnp.max(s, axis=1, keepdims=True)
    p = jnp.exp(s - m)
    den = jnp.sum(p, axis=1, keepdims=True)
    pv = jnp.dot(p.astype(BF16), vv, preferred_element_type=F32)
    return jnp.where(head0, pv[:w], pv[w:]), m, den


def _attn_kernel(q_ref, k_ref, v_ref, o_ref, m_ref, den_ref, *, dilation, groups, nblk):
    w = ATTN_BLOCK
    lane = lax.broadcasted_iota(jnp.int32, (w, LANES), 1)
    qi1 = lax.broadcasted_iota(jnp.int32, (2 * w, w), 0) % w
    kj1 = lax.broadcasted_iota(jnp.int32, (2 * w, w), 1)
    valid_first = kj1 <= qi1
    qi2 = lax.broadcasted_iota(jnp.int32, (2 * w, 2 * w), 0) % w
    kj2 = lax.broadcasted_iota(jnp.int32, (2 * w, 2 * w), 1)
    dist = qi2 + w - kj2
    valid_band = (dist >= 0) & (dist <= w)
    res0 = pl.program_id(1) * groups

    def do_block(g, qb, r0, rows_k, valid):
        rows_q = pl.ds(r0, w)
        m_acc = jnp.zeros((w, LANES), F32)
        d_acc = jnp.zeros((w, LANES), F32)
        for hp in range(PAIRS):
            lanes = slice(hp * LANES, (hp + 1) * LANES)
            o, m, den = _attn_block(q_ref[0, g, rows_q, lanes], k_ref[0, g, rows_k, lanes],
                                    v_ref[0, g, rows_k, lanes], valid)
            if dilation == 1:
                o_ref[0, hp, rows_q, :] = o
            else:
                tok0 = r0 * dilation + res0 + g
                o_ref[0, hp, pl.ds(tok0, w, stride=dilation), :] = o
            for h in range(2):
                sel = lane == 2 * hp + h
                m_acc = jnp.where(sel, m[h * w:(h + 1) * w], m_acc)
                d_acc = jnp.where(sel, den[h * w:(h + 1) * w], d_acc)
        m_ref[0, g, qb] = m_acc.T[:ATTN_HEADS, :]
        den_ref[0, g, qb] = d_acc.T[:ATTN_HEADS, :]

    for g in range(groups):
        do_block(g, 0, 0, pl.ds(0, w), valid_first)
        if nblk > 1:
            def body(qb, carry, g=g):
                r0 = pl.multiple_of(qb * w, w)
                do_block(g, qb, r0, pl.ds(r0 - w, 2 * w), valid_band)
                return carry
            lax.fori_loop(1, nblk, body, 0)


def _attention(q, k, v, dilation, groups):
    b, _, l, a = q.shape
    s = l * dilation
    nblk = l // ATTN_BLOCK
    spec = pl.BlockSpec((1, groups, l, a), lambda i, r: (i, r, 0, 0))
    stat_spec = pl.BlockSpec((1, groups, nblk, ATTN_HEADS, ATTN_BLOCK), lambda i, r: (i, r, 0, 0, 0))
    stat_shape = jax.ShapeDtypeStruct((b, dilation, nblk, ATTN_HEADS, ATTN_BLOCK), F32)
    o, m, den = pl.pallas_call(
        functools.partial(_attn_kernel, dilation=dilation, groups=groups, nblk=nblk),
        grid=(b, dilation // groups),
        in_specs=[spec, spec, spec],
        out_specs=[pl.BlockSpec((1, PAIRS, s, LANES), lambda i, r: (i, 0, 0, 0)),
                   stat_spec, stat_spec],
        out_shape=[jax.ShapeDtypeStruct((b, PAIRS, s, LANES), F32), stat_shape, stat_shape],
        compiler_params=pltpu.CompilerParams(
            dimension_semantics=("arbitrary", "arbitrary"), vmem_limit_bytes=VMEM_LIMIT),
        name=f"attn_d{dilation}",
    )(q, k, v)
    nat = lambda t: t.transpose(0, 2, 4, 1, 3).reshape(b, s, ATTN_HEADS)
    return o, nat(m), nat(den)


def _hgrn_kernel(lbl_ref, q_ref, f_ref, i_ref, g_ref, nw_ref, o_ref,
                 qd_scr, kd_scr, kl_scr, att_scr, cs_scr, dec_scr, st_scr, *, seq, layer):
    c = HG_CHUNK
    t2 = 2 * c
    ntile = seq // t2
    lg = lbl_ref[...]
    e = jnp.exp(lg - jnp.max(lg, axis=0, keepdims=True))
    lb = jnp.sum(e[0:layer + 1, :], axis=0, keepdims=True) / jnp.sum(e, axis=0, keepdims=True)

    row = lax.broadcasted_iota(jnp.int32, (t2, HG_DIM), 0)
    rin = row % c
    col = lax.broadcasted_iota(jnp.int32, (t2, t2), 1)
    rr = lax.broadcasted_iota(jnp.int32, (t2, t2), 0)
    tril = ((rr // c) == (col // c)) & ((col % c) <= (rr % c))

    def decay_step(t, carry):
        r0 = pl.multiple_of(t * t2, t2)
        rows = pl.ds(r0, t2)
        f = lb + (1.0 - lb) * (1.0 / (1.0 + jnp.exp(-f_ref[0, rows, :])))
        gl = jnp.log(f)
        kk = 1.0 - f
        for sft in (1, 2, 4, 8, 16, 32):
            gl = gl + jnp.where(rin >= sft, pltpu.roll(gl, sft, axis=0), 0.0)
        g_last = jnp.where(row < c, gl[c - 1:c, :], gl[t2 - 1:t2, :])
        qd_scr[rows, :] = (q_ref[0, rows, :].astype(F32) * jnp.exp(gl)).astype(BF16)
        kd_scr[rows, :] = (kk * jnp.exp(-gl)).astype(BF16)
        kl_scr[rows, :] = (kk * jnp.exp(g_last - gl)).astype(BF16)
        for j in range(2):
            dec_scr[2 * t + j] = jnp.exp(gl[(j + 1) * c - 1:(j + 1) * c, :])
        return carry

    lax.fori_loop(0, ntile, decay_step, 0, unroll=4)

    first_chunk = row < c

    def score_step(t, carry):
        r0 = pl.multiple_of(t * t2, t2)
        rows = pl.ds(r0, t2)
        att = lax.dot_general(qd_scr[rows, :], kd_scr[rows, :], (((1,), (1,)), ((), ())),
                              preferred_element_type=F32)
        att_scr[rows, :] = jnp.where(tril, att, 0.0).astype(BF16)
        kl = kl_scr[rows, :]
        zero = jnp.zeros_like(kl)
        kl2 = jnp.concatenate([jnp.where(first_chunk, kl, zero), jnp.where(first_chunk, zero, kl)],
                              axis=1)
        cs2 = lax.dot_general(i_ref[0, rows, :], kl2, (((0,), (0,)), ((), ())),
                              preferred_element_type=F32)
        cs_scr[2 * t] = cs2[:, :HG_DIM]
        cs_scr[2 * t + 1] = cs2[:, HG_DIM:]
        return carry

    lax.fori_loop(0, ntile, score_step, 0, unroll=4)

    def scan_step(ch, state_t):
        st_scr[pl.ds(pl.multiple_of(ch * HG_DIM, HG_DIM), HG_DIM), :] = state_t.astype(BF16)
        return state_t * dec_scr[ch] + cs_scr[ch]

    lax.fori_loop(0, seq // c, scan_step, jnp.zeros((HG_DIM, HG_DIM), F32), unroll=4)

    nw = nw_ref[...]

    def out_step(t, carry):
        r0 = pl.multiple_of(t * t2, t2)
        rows = pl.ds(r0, t2)
        intra = jnp.dot(att_scr[rows, :], i_ref[0, rows, :], preferred_element_type=F32)
        states = st_scr[pl.ds(pl.multiple_of(t * 2 * HG_DIM, 2 * HG_DIM), 2 * HG_DIM), :]
        inter2 = lax.dot_general(qd_scr[rows, :], states, (((1,), (1,)), ((), ())),
                                 preferred_element_type=F32)
        rec = intra + jnp.where(first_chunk, inter2[:, :HG_DIM], inter2[:, HG_DIM:])
        gate = g_ref[0, rows, :].astype(F32)
        o_ref[0, rows, :] = (_rms_norm_rows(rec, nw) * _silu(gate)).astype(BF16)
        return carry

    lax.fori_loop(0, ntile, out_step, 0, unroll=4)


def _hgrn2(lb_logits, hq, hf, hi, hg, nw, layer):
    b, s, _ = hq.shape
    nl = lb_logits.shape[0]
    nchunk = s // HG_CHUNK
    spec = pl.BlockSpec((1, s, HG_DIM), lambda i, h: (i, 0, h))
    return pl.pallas_call(
        functools.partial(_hgrn_kernel, seq=s, layer=layer),
        grid=(b, HG_HEADS),
        in_specs=[pl.BlockSpec((nl, HG_DIM), lambda i, h: (0, h)),
                  spec, spec, spec, spec,
                  pl.BlockSpec((1, HG_DIM), lambda i, h: (0, 0))],
        out_specs=spec,
        out_shape=jax.ShapeDtypeStruct((b, s, HG_WIDTH), BF16),
        scratch_shapes=[pltpu.VMEM((s, HG_DIM), BF16),
                        pltpu.VMEM((s, HG_DIM), BF16),
                        pltpu.VMEM((s, HG_DIM), BF16),
                        pltpu.VMEM((s, 2 * HG_CHUNK), BF16),
                        pltpu.VMEM((nchunk, HG_DIM, HG_DIM), F32),
                        pltpu.VMEM((nchunk, 1, HG_DIM), F32),
                        pltpu.VMEM((nchunk * HG_DIM, HG_DIM), BF16)],
        compiler_params=pltpu.CompilerParams(
            dimension_semantics=("arbitrary", "arbitrary"), vmem_limit_bytes=VMEM_LIMIT),
        name="hgrn2",
    )(lb_logits, hq, hf, hi, hg, nw)


def _out_ffn2_kernel(x_ref, o1_ref, o2_ref, o3_ref, m1_ref, m2_ref, m3_ref,
                     d1_ref, d2_ref, d3_ref, rec_ref,
                     wo_ref, nw_ref, w1_ref, w3_ref, w2_ref, out_ref):
    o_refs = (o1_ref, o2_ref, o3_ref)
    ms = [r[0] for r in (m1_ref, m2_ref, m3_ref)]
    dens = [r[0] for r in (d1_ref, d2_ref, d3_ref)]
    mx = jnp.maximum(jnp.maximum(ms[0], ms[1]), ms[2])
    es = [jnp.exp(m - mx) for m in ms]
    inv = 1.0 / (es[0] * dens[0] + es[1] * dens[1] + es[2] * dens[2])
    head = lax.broadcasted_iota(jnp.int32, (ATTN_HEADS, ATTN_WIDTH), 0)
    lane_head = lax.broadcasted_iota(jnp.int32, (ATTN_HEADS, ATTN_WIDTH), 1) // HEAD_DIM
    expand = (head == lane_head).astype(BF16)
    wide = []
    for e in es:
        hi, lo = _split_bf16(e * inv)
        wide.append(jnp.dot(hi, expand, preferred_element_type=F32)
                    + jnp.dot(lo, expand, preferred_element_type=F32))
    parts = []
    for hp in range(PAIRS):
        lanes = slice(hp * LANES, (hp + 1) * LANES)
        acc = wide[0][:, lanes] * o_refs[0][0, hp]
        for p in (1, 2):
            acc = acc + wide[p][:, lanes] * o_refs[p][0, hp]
        parts.append(acc.astype(BF16))
    mixed = jnp.concatenate(parts + [rec_ref[0]], axis=1)
    x2 = x_ref[0] + jnp.dot(mixed, wo_ref[...], preferred_element_type=F32)
    out_ref[0] = _swiglu_half_step(x2, nw_ref[...], w1_ref, w3_ref, w2_ref)


def _out_ffn2(x3d, os_, ms, dens, rec, wo, nw, w1, w3, w2, tm):
    b, s, d = x3d.shape
    f = w1.shape[1]
    a = ATTN_WIDTH
    row = lambda i, j: (i, j, 0)
    return pl.pallas_call(
        _out_ffn2_kernel,
        grid=(b, s // tm),
        in_specs=[pl.BlockSpec((1, tm, d), row)]
                 + [pl.BlockSpec((1, PAIRS, tm, LANES), lambda i, j: (i, 0, j, 0))] * 3
                 + [pl.BlockSpec((1, tm, ATTN_HEADS), row)] * 6
                 + [pl.BlockSpec((1, tm, HG_WIDTH), row),
                    _const_spec((a + HG_WIDTH, d)),
                    _const_spec((1, d)),
                    _const_spec((d, f)),
                    _const_spec((d, f)),
                    _const_spec((f, d))],
        out_specs=pl.BlockSpec((1, tm, d), row),
        out_shape=jax.ShapeDtypeStruct((b, s, d), F32),
        compiler_params=pltpu.CompilerParams(
            dimension_semantics=("arbitrary", "arbitrary"), vmem_limit_bytes=VMEM_LIMIT),
        name="out_ffn2",
    )(x3d, *os_, *ms, *dens, rec, wo, nw, w1, w3, w2)


def _rope_lane_tables(s):
    half = ROPE_DIM // 2
    inv = ROPE_THETA ** (-jnp.arange(0, ROPE_DIM, 2, dtype=F32) / ROPE_DIM)
    ang = jnp.arange(s, dtype=F32)[:, None] * inv[None, :]
    cos, sin = jnp.cos(ang), jnp.sin(ang)
    dim = jnp.arange(LANES) % HEAD_DIM
    idx = dim % half
    c = jnp.where(dim[None, :] < ROPE_DIM, cos[:, idx], 1.0)
    s_lo = jnp.where(dim[None, :] < half, -sin[:, idx], 0.0)
    s_hi = jnp.where((dim[None, :] >= half) & (dim[None, :] < ROPE_DIM), sin[:, idx], 0.0)
    return c.astype(F32), s_lo.astype(F32), s_hi.astype(F32)


def kernel(x, ffn1_norm, ffn1_w1, ffn1_w3, ffn1_w2, mix_norm, w_in, q_norm, k_norm,
           hg_lb_logits, hg_out_norm, w_out, ffn2_norm, ffn2_w1, ffn2_w3, ffn2_w2):
    b, s, d = x.shape
    depth = ffn1_norm.shape[0]
    tm = 512
    cos, s_lo, s_hi = _rope_lane_tables(s)
    head_of = jnp.arange(2 * LANES) // HEAD_DIM
    seg = (head_of[:, None] == head_of[None, :]).astype(BF16)
    for layer in range(depth):
        x1 = _ffn1(x.reshape(b * s, d), ffn1_norm[layer][None, :], ffn1_w1[layer].astype(BF16),
                   ffn1_w3[layer].astype(BF16), ffn1_w2[layer].astype(BF16), tm).reshape(b, s, d)
        (q1, k1, v1, q4, k4, v4, q16, k16, v16, hq, hf, hi, hg) = _in_proj(
            x1, mix_norm[layer][None, :], w_in[layer].astype(BF16), seg,
            jnp.tile(q_norm[layer], ATTN_HEADS)[None, :], jnp.tile(k_norm[layer], ATTN_HEADS)[None, :],
            cos, s_lo, s_hi, tm)
        os_, ms, dens = [], [], []
        for dilation, groups, (q, k, v) in zip(
                DILATIONS, (1, 1, 4),
                ((q1[:, None], k1[:, None], v1[:, None]), (q4, k4, v4), (q16, k16, v16))):
            o, m, den = _attention(q, k, v, dilation, groups)
            os_.append(o)
            ms.append(m)
            dens.append(den)
        rec = _hgrn2(hg_lb_logits, hq, hf, hi, hg, hg_out_norm[layer][None, :], layer)
        x = _out_ffn2(x1, os_, ms, dens, rec, w_out[layer].astype(BF16),
                      ffn2_norm[layer][None, :], ffn2_w1[layer].astype(BF16),
                      ffn2_w3[layer].astype(BF16), ffn2_w2[layer].astype(BF16), tm)
    return x
```

```python
import functools

import jax
import jax.numpy as jnp
from jax import lax
from jax.experimental import pallas as pl
from jax.experimental.pallas import tpu as pltpu

F32 = jnp.float32
BF16 = jnp.bfloat16

EPS = 1e-6
NEG_INF = -1e30
LOG2_E = 1.4426950408889634
HEAD_DIM = 64
ATTN_HEADS = 8
ATTN_WIDTH = ATTN_HEADS * HEAD_DIM
ROPE_DIM = HEAD_DIM // 4
ROPE_THETA = 500000.0
DILATIONS = (1, 4, 16)
HG_HEADS = 4
HG_DIM = 128
HG_CHUNK = 64
HG_WIDTH = HG_HEADS * HG_DIM

LANES = 128
ATTN_BLOCK = 128
PAIRS = ATTN_WIDTH // LANES
ATTN_UNROLL = 3
VMEM_LIMIT = 56 * 1024 * 1024


def _const_spec(shape):
    nd = len(shape)
    return pl.BlockSpec(shape, lambda *_: (0,) * nd, pipeline_mode=pl.Buffered(1))


def _rms_norm_rows(x, w):
    return x * lax.rsqrt(jnp.mean(x * x, axis=-1, keepdims=True) + EPS) * w


def _silu(a):
    return a * (1.0 / (1.0 + jnp.exp(-a)))


def _swiglu_half_step(x, nw, w1_ref, w3_ref, w2_ref):
    h = _rms_norm_rows(x, nw).astype(BF16)
    a = jnp.dot(h, w1_ref[...], preferred_element_type=F32)
    b = jnp.dot(h, w3_ref[...], preferred_element_type=F32)
    g = (_silu(a) * b).astype(BF16)
    y = jnp.dot(g, w2_ref[...], preferred_element_type=F32)
    return x + 0.5 * y


def _split_bf16(x):
    hi = x.astype(BF16)
    return hi, (x - hi.astype(F32)).astype(BF16)


def _ffn1_kernel(x_ref, nw_ref, w1_ref, w3_ref, w2_ref, o_ref):
    o_ref[...] = _swiglu_half_step(x_ref[...], nw_ref[...], w1_ref, w3_ref, w2_ref)


def _ffn1(x2d, nw, w1, w3, w2, tm):
    n, d = x2d.shape
    f = w1.shape[1]
    return pl.pallas_call(
        _ffn1_kernel,
        grid=(n // tm,),
        in_specs=[
            pl.BlockSpec((tm, d), lambda i: (i, 0)),
            _const_spec((1, d)),
            _const_spec((d, f)),
            _const_spec((d, f)),
            _const_spec((f, d)),
        ],
        out_specs=pl.BlockSpec((tm, d), lambda i: (i, 0)),
        out_shape=jax.ShapeDtypeStruct((n, d), F32),
        compiler_params=pltpu.CompilerParams(
            dimension_semantics=("arbitrary",), vmem_limit_bytes=VMEM_LIMIT),
        name="ffn1",
    )(x2d, nw, w1, w3, w2)


def _head_norm_rope(t, seg_ref, w, cos, sin_lo, sin_hi, scale):
    hi, lo = _split_bf16(t * t)
    seg = seg_ref[...]
    sw = seg.shape[0]
    ms = jnp.concatenate(
        [jnp.dot(hi[:, c:c + sw], seg, preferred_element_type=F32)
         + jnp.dot(lo[:, c:c + sw], seg, preferred_element_type=F32)
         for c in range(0, ATTN_WIDTH, sw)], axis=1) * (1.0 / HEAD_DIM)
    y = t * lax.rsqrt(ms + EPS) * w
    outs = []
    for c in range(PAIRS):
        yc = y[:, c * LANES:(c + 1) * LANES]
        from_hi = pltpu.roll(yc, LANES - ROPE_DIM // 2, axis=1)
        from_lo = pltpu.roll(yc, ROPE_DIM // 2, axis=1)
        outs.append((yc * cos + from_hi * sin_lo + from_lo * sin_hi) * scale)
    return outs


def _in_proj_kernel(x_ref, nw_ref, w_ref, seg_ref, qw_ref, kw_ref, cos_ref, slo_ref, shi_ref,
                    q1_ref, k1_ref, v1_ref, q4_ref, k4_ref, v4_ref, q16_ref, k16_ref, v16_ref,
                    hq_ref, hf_ref, hi_ref, hg_ref, slab_scr, res4_scr):
    tm = x_ref.shape[0]
    h = _rms_norm_rows(x_ref[...], nw_ref[...]).astype(BF16)
    p = jnp.dot(h, w_ref[...], preferred_element_type=F32)
    a = ATTN_WIDTH
    cos, slo, shi = cos_ref[...], slo_ref[...], shi_ref[...]
    qs = _head_norm_rope(p[:, 0:a], seg_ref, qw_ref[...], cos, slo, shi, LOG2_E * HEAD_DIM ** -0.5)
    ks = _head_norm_rope(p[:, a:2 * a], seg_ref, kw_ref[...], cos, slo, shi, 1.0)
    vs = [p[:, 2 * a + c * LANES:2 * a + (c + 1) * LANES] for c in range(PAIRS)]
    outs = ((q1_ref, q4_ref, q16_ref), (k1_ref, k4_ref, k16_ref), (v1_ref, v4_ref, v16_ref))
    for ti, slabs in enumerate((qs, ks, vs)):
        o1, o4, o16 = outs[ti]
        for c in range(PAIRS):
            lanes = slice(c * LANES, (c + 1) * LANES)
            o1[0, :, lanes] = slabs[c].astype(BF16)
            sl = ti * PAIRS + c
            slab_scr[sl] = slabs[c]
            for r in range(4):
                rows4 = slab_scr[sl, pl.ds(r, tm // 4, stride=4), :]
                o4[0, r, :, lanes] = rows4.astype(BF16)
                res4_scr[sl, r] = rows4
                for a_ in range(4):
                    rows16 = res4_scr[sl, r, pl.ds(a_, tm // 16, stride=4), :]
                    o16[0, 4 * a_ + r, :, lanes] = rows16.astype(BF16)
    o = 3 * a
    hq_ref[0] = p[:, o:o + HG_WIDTH].astype(BF16)
    hf_ref[0] = p[:, o + HG_WIDTH:o + 2 * HG_WIDTH]
    hi_ref[0] = p[:, o + 2 * HG_WIDTH:o + 3 * HG_WIDTH].astype(BF16)
    hg_ref[0] = p[:, o + 3 * HG_WIDTH:o + 4 * HG_WIDTH].astype(BF16)


def _in_proj(x3d, nw, w_in, seg, qw, kw, cos, slo, shi, tm):
    b, s, d = x3d.shape
    cols = w_in.shape[1]
    a = ATTN_WIDTH
    row = lambda i, j: (i, j, 0)
    pos = lambda i, j: (j, 0)
    out_specs, out_shape = [], []
    for dil in DILATIONS:
        for _ in range(3):
            if dil == 1:
                out_specs.append(pl.BlockSpec((1, tm, a), row))
                out_shape.append(jax.ShapeDtypeStruct((b, s, a), BF16))
            else:
                out_specs.append(pl.BlockSpec((1, dil, tm // dil, a), lambda i, j: (i, 0, j, 0)))
                out_shape.append(jax.ShapeDtypeStruct((b, dil, s // dil, a), BF16))
    for dt in (BF16, F32, BF16, BF16):
        out_specs.append(pl.BlockSpec((1, tm, HG_WIDTH), row))
        out_shape.append(jax.ShapeDtypeStruct((b, s, HG_WIDTH), dt))
    return pl.pallas_call(
        _in_proj_kernel,
        grid=(b, s // tm),
        in_specs=[
            pl.BlockSpec((None, tm, d), row),
            _const_spec((1, d)),
            _const_spec((d, cols)),
            _const_spec(seg.shape),
            _const_spec((1, a)),
            _const_spec((1, a)),
            pl.BlockSpec((tm, LANES), pos),
            pl.BlockSpec((tm, LANES), pos),
            pl.BlockSpec((tm, LANES), pos),
        ],
        out_specs=out_specs,
        out_shape=out_shape,
        scratch_shapes=[pltpu.VMEM((3 * PAIRS, tm, LANES), F32),
                        pltpu.VMEM((3 * PAIRS, 4, tm // 4, LANES), F32)],
        compiler_params=pltpu.CompilerParams(
            dimension_semantics=("arbitrary", "arbitrary"), vmem_limit_bytes=VMEM_LIMIT),
        name="in_proj",
    )(x3d, nw, w_in, seg, qw, kw, cos, slo, shi)


def _attn_kernel(q_ref, k_ref, v_ref, o_ref, m_ref, den_ref, p_scr, *, dilation, groups, nblk):
    w = ATTN_BLOCK
    lane = lax.broadcasted_iota(jnp.int32, (w, LANES), 1)
    head0 = lane < HEAD_DIM
    qi1 = lax.broadcasted_iota(jnp.int32, (2 * w, w), 0) % w
    kj1 = lax.broadcasted_iota(jnp.int32, (2 * w, w), 1)
    valid_first = kj1 <= qi1
    qi2 = lax.broadcasted_iota(jnp.int32, (2 * w, 2 * w), 0) % w
    kj2 = lax.broadcasted_iota(jnp.int32, (2 * w, 2 * w), 1)
    dist = qi2 + w - kj2
    valid_band = (dist >= 0) & (dist <= w)
    res0 = pl.program_id(1) * groups

    def to_heads_major(acc):
        return acc.T[:ATTN_HEADS, :]

    def score_block(g, qb, r0, rows_k, nk, valid):
        rows_q = pl.ds(r0, w)
        m_acc = jnp.zeros((w, LANES), F32)
        for hp in range(PAIRS):
            lanes = slice(hp * LANES, (hp + 1) * LANES)
            q2 = q_ref[0, g, rows_q, lanes]
            zero = jnp.zeros_like(q2)
            qq = jnp.concatenate([jnp.where(head0, q2, zero), jnp.where(head0, zero, q2)], axis=0)
            s = lax.dot_general(qq, k_ref[0, g, rows_k, lanes], (((1,), (1,)), ((), ())),
                                preferred_element_type=F32)
            s = jnp.where(valid, s, NEG_INF)
            m = jnp.max(s, axis=1, keepdims=True)
            p_scr[g * nblk + qb, hp, :, 0:nk] = jnp.exp2(s - m).astype(BF16)
            for h in range(2):
                m_acc = jnp.where(lane == 2 * hp + h, m[h * w:(h + 1) * w], m_acc)
        m_ref[0, g, qb] = to_heads_major(m_acc)

    def value_block(g, qb, r0, rows_k, nk):
        d_acc = jnp.zeros((w, LANES), F32)
        ones = jnp.ones((nk, LANES), BF16)
        for hp in range(PAIRS):
            lanes = slice(hp * LANES, (hp + 1) * LANES)
            vx = jnp.concatenate([v_ref[0, g, rows_k, lanes], ones], axis=1)
            r = jnp.dot(p_scr[g * nblk + qb, hp, :, 0:nk], vx, preferred_element_type=F32)
            o = jnp.where(head0, r[:w, :LANES], r[w:, :LANES])
            if dilation == 1:
                o_ref[0, hp, pl.ds(r0, w), :] = o
            else:
                tok0 = r0 * dilation + res0 + g
                o_ref[0, hp, pl.ds(tok0, w, stride=dilation), :] = o
            for h in range(2):
                d_acc = jnp.where(lane == 2 * hp + h, r[h * w:(h + 1) * w, LANES:], d_acc)
        den_ref[0, g, qb] = to_heads_major(d_acc)

    def band_rows(qb):
        r0 = pl.multiple_of(qb * w, w)
        return r0, pl.ds(r0 - w, 2 * w)

    for g in range(groups):
        score_block(g, 0, 0, pl.ds(0, w), w, valid_first)
    if nblk > 1:
        for g in range(groups):
            def score_body(qb, carry, g=g):
                r0, rows_k = band_rows(qb)
                score_block(g, qb, r0, rows_k, 2 * w, valid_band)
                return carry
            lax.fori_loop(1, nblk, score_body, 0, unroll=ATTN_UNROLL)
    for g in range(groups):
        value_block(g, 0, 0, pl.ds(0, w), w)
    if nblk > 1:
        for g in range(groups):
            def value_body(qb, carry, g=g):
                r0, rows_k = band_rows(qb)
                value_block(g, qb, r0, rows_k, 2 * w)
                return carry
            lax.fori_loop(1, nblk, value_body, 0, unroll=ATTN_UNROLL)


def _attention(q, k, v, dilation, groups):
    b, _, l, a = q.shape
    s = l * dilation
    nblk = l // ATTN_BLOCK
    spec = pl.BlockSpec((1, groups, l, a), lambda i, r: (i, r, 0, 0))
    stat_spec = pl.BlockSpec((1, groups, nblk, ATTN_HEADS, ATTN_BLOCK), lambda i, r: (i, r, 0, 0, 0))
    stat_shape = jax.ShapeDtypeStruct((b, dilation, nblk, ATTN_HEADS, ATTN_BLOCK), F32)
    o, m, den = pl.pallas_call(
        functools.partial(_attn_kernel, dilation=dilation, groups=groups, nblk=nblk),
        grid=(b, dilation // groups),
        in_specs=[spec, spec, spec],
        out_specs=[pl.BlockSpec((1, PAIRS, s, LANES), lambda i, r: (i, 0, 0, 0)),
                   stat_spec, stat_spec],
        out_shape=[jax.ShapeDtypeStruct((b, PAIRS, s, LANES), F32), stat_shape, stat_shape],
        scratch_shapes=[pltpu.VMEM((groups * nblk, PAIRS, 2 * ATTN_BLOCK, 2 * ATTN_BLOCK), BF16)],
        compiler_params=pltpu.CompilerParams(
            dimension_semantics=("arbitrary", "arbitrary"), vmem_limit_bytes=VMEM_LIMIT),
        name=f"attn_d{dilation}",
    )(q, k, v)
    nat = lambda t: t.transpose(0, 2, 4, 1, 3).reshape(b, s, ATTN_HEADS)
    return o, nat(m), nat(den)


def _hgrn_kernel(lbl_ref, q_ref, f_ref, i_ref, g_ref, nw_ref, o_ref,
                 qd_scr, kd_scr, kl_scr, att_scr, cs_scr, dec_scr, st_scr, *, seq, layer):
    c = HG_CHUNK
    t2 = 2 * c
    ntile = seq // t2
    lg = lbl_ref[...]
    e = jnp.exp(lg - jnp.max(lg, axis=0, keepdims=True))
    lb = jnp.sum(e[0:layer + 1, :], axis=0, keepdims=True) / jnp.sum(e, axis=0, keepdims=True)

    row = lax.broadcasted_iota(jnp.int32, (t2, HG_DIM), 0)
    rin = row % c
    col = lax.broadcasted_iota(jnp.int32, (t2, t2), 1)
    rr = lax.broadcasted_iota(jnp.int32, (t2, t2), 0)
    tril = ((rr // c) == (col // c)) & ((col % c) <= (rr % c))

    def decay_step(t, carry):
        r0 = pl.multiple_of(t * t2, t2)
        rows = pl.ds(r0, t2)
        f = lb + (1.0 - lb) * (1.0 / (1.0 + jnp.exp(-f_ref[0, rows, :])))
        gl = jnp.log(f)
        kk = 1.0 - f
        for sft in (1, 2, 4, 8, 16, 32):
            gl = gl + jnp.where(rin >= sft, pltpu.roll(gl, sft, axis=0), 0.0)
        g_last = jnp.where(row < c, gl[c - 1:c, :], gl[t2 - 1:t2, :])
        qd_scr[rows, :] = (q_ref[0, rows, :].astype(F32) * jnp.exp(gl)).astype(BF16)
        kd_scr[rows, :] = (kk * jnp.exp(-gl)).astype(BF16)
        kl_scr[rows, :] = (kk * jnp.exp(g_last - gl)).astype(BF16)
        for j in range(2):
            dec_scr[2 * t + j] = jnp.exp(gl[(j + 1) * c - 1:(j + 1) * c, :])
        return carry

    lax.fori_loop(0, ntile, decay_step, 0, unroll=4)

    first_chunk = row < c

    def score_step(t, carry):
        r0 = pl.multiple_of(t * t2, t2)
        rows = pl.ds(r0, t2)
        att = lax.dot_general(qd_scr[rows, :], kd_scr[rows, :], (((1,), (1,)), ((), ())),
                              preferred_element_type=F32)
        att_scr[rows, :] = jnp.where(tril, att, 0.0).astype(BF16)
        kl = kl_scr[rows, :]
        zero = jnp.zeros_like(kl)
        kl2 = jnp.concatenate([jnp.where(first_chunk, kl, zero), jnp.where(first_chunk, zero, kl)],
                              axis=1)
        cs2 = lax.dot_general(i_ref[0, rows, :], kl2, (((0,), (0,)), ((), ())),
                              preferred_element_type=F32)
        cs_scr[2 * t] = cs2[:, :HG_DIM]
        cs_scr[2 * t + 1] = cs2[:, HG_DIM:]
        return carry

    lax.fori_loop(0, ntile, score_step, 0, unroll=4)

    def scan_step(ch, state_t):
        st_scr[pl.ds(pl.multiple_of(ch * HG_DIM, HG_DIM), HG_DIM), :] = state_t.astype(BF16)
        return state_t * dec_scr[ch] + cs_scr[ch]

    lax.fori_loop(0, seq // c, scan_step, jnp.zeros((HG_DIM, HG_DIM), F32), unroll=4)

    nw = nw_ref[...]

    def out_step(t, carry):
        r0 = pl.multiple_of(t * t2, t2)
        rows = pl.ds(r0, t2)
        intra = jnp.dot(att_scr[rows, :], i_ref[0, rows, :], preferred_element_type=F32)
        states = st_scr[pl.ds(pl.multiple_of(t * 2 * HG_DIM, 2 * HG_DIM), 2 * HG_DIM), :]
        inter2 = lax.dot_general(qd_scr[rows, :], states, (((1,), (1,)), ((), ())),
                                 preferred_element_type=F32)
        rec = intra + jnp.where(first_chunk, inter2[:, :HG_DIM], inter2[:, HG_DIM:])
        gate = g_ref[0, rows, :].astype(F32)
        o_ref[0, rows, :] = (_rms_norm_rows(rec, nw) * _silu(gate)).astype(BF16)
        return carry

    lax.fori_loop(0, ntile, out_step, 0, unroll=4)


def _hgrn2(lb_logits, hq, hf, hi, hg, nw, layer):
    b, s, _ = hq.shape
    nl = lb_logits.shape[0]
    nchunk = s // HG_CHUNK
    spec = pl.BlockSpec((1, s, HG_DIM), lambda i, h: (i, 0, h))
    return pl.pallas_call(
        functools.partial(_hgrn_kernel, seq=s, layer=layer),
        grid=(b, HG_HEADS),
        in_specs=[pl.BlockSpec((nl, HG_DIM), lambda i, h: (0, h)),
                  spec, spec, spec, spec,
                  pl.BlockSpec((1, HG_DIM), lambda i, h: (0, 0))],
        out_specs=spec,
        out_shape=jax.ShapeDtypeStruct((b, s, HG_WIDTH), BF16),
        scratch_shapes=[pltpu.VMEM((s, HG_DIM), BF16),
                        pltpu.VMEM((s, HG_DIM), BF16),
                        pltpu.VMEM((s, HG_DIM), BF16),
                        pltpu.VMEM((s, 2 * HG_CHUNK), BF16),
                        pltpu.VMEM((nchunk, HG_DIM, HG_DIM), F32),
                        pltpu.VMEM((nchunk, 1, HG_DIM), F32),
                        pltpu.VMEM((nchunk * HG_DIM, HG_DIM), BF16)],
        compiler_params=pltpu.CompilerParams(
            dimension_semantics=("arbitrary", "arbitrary"), vmem_limit_bytes=VMEM_LIMIT),
        name="hgrn2",
    )(lb_logits, hq, hf, hi, hg, nw)


def _out_ffn2_kernel(x_ref, o1_ref, o2_ref, o3_ref, m1_ref, m2_ref, m3_ref,
                     d1_ref, d2_ref, d3_ref, rec_ref,
                     wo_ref, nw_ref, w1_ref, w3_ref, w2_ref, out_ref):
    o_refs = (o1_ref, o2_ref, o3_ref)
    ms = [r[0] for r in (m1_ref, m2_ref, m3_ref)]
    dens = [r[0] for r in (d1_ref, d2_ref, d3_ref)]
    mx = jnp.maximum(jnp.maximum(ms[0], ms[1]), ms[2])
    es = [jnp.exp2(m - mx) for m in ms]
    inv = 1.0 / (es[0] * dens[0] + es[1] * dens[1] + es[2] * dens[2])
    head = lax.broadcasted_iota(jnp.int32, (ATTN_HEADS, ATTN_WIDTH), 0)
    lane_head = lax.broadcasted_iota(jnp.int32, (ATTN_HEADS, ATTN_WIDTH), 1) // HEAD_DIM
    expand = (head == lane_head).astype(BF16)
    wide = []
    for e in es:
        hi, lo = _split_bf16(e * inv)
        wide.append(jnp.dot(hi, expand, preferred_element_type=F32)
                    + jnp.dot(lo, expand, preferred_element_type=F32))
    parts = []
    for hp in range(PAIRS):
        lanes = slice(hp * LANES, (hp + 1) * LANES)
        acc = wide[0][:, lanes] * o_refs[0][0, hp]
        for p in (1, 2):
            acc = acc + wide[p][:, lanes] * o_refs[p][0, hp]
        parts.append(acc.astype(BF16))
    mixed = jnp.concatenate(parts + [rec_ref[0]], axis=1)
    x2 = x_ref[0] + jnp.dot(mixed, wo_ref[...], preferred_element_type=F32)
    out_ref[0] = _swiglu_half_step(x2, nw_ref[...], w1_ref, w3_ref, w2_ref)


def _out_ffn2(x3d, os_, ms, dens, rec, wo, nw, w1, w3, w2, tm):
    b, s, d = x3d.shape
    f = w1.shape[1]
    a = ATTN_WIDTH
    row = lambda i, j: (i, j, 0)
    return pl.pallas_call(
        _out_ffn2_kernel,
        grid=(b, s // tm),
        in_specs=[pl.BlockSpec((1, tm, d), row)]
                 + [pl.BlockSpec((1, PAIRS, tm, LANES), lambda i, j: (i, 0, j, 0))] * 3
                 + [pl.BlockSpec((1, tm, ATTN_HEADS), row)] * 6
                 + [pl.BlockSpec((1, tm, HG_WIDTH), row),
                    _const_spec((a + HG_WIDTH, d)),
                    _const_spec((1, d)),
                    _const_spec((d, f)),
                    _const_spec((d, f)),
                    _const_spec((f, d))],
        out_specs=pl.BlockSpec((1, tm, d), row),
        out_shape=jax.ShapeDtypeStruct((b, s, d), F32),
        compiler_params=pltpu.CompilerParams(
            dimension_semantics=("arbitrary", "arbitrary"), vmem_limit_bytes=VMEM_LIMIT),
        name="out_ffn2",
    )(x3d, *os_, *ms, *dens, rec, wo, nw, w1, w3, w2)


def _rope_lane_tables(s):
    half = ROPE_DIM // 2
    inv = ROPE_THETA ** (-jnp.arange(0, ROPE_DIM, 2, dtype=F32) / ROPE_DIM)
    ang = jnp.arange(s, dtype=F32)[:, None] * inv[None, :]
    cos, sin = jnp.cos(ang), jnp.sin(ang)
    dim = jnp.arange(LANES) % HEAD_DIM
    idx = dim % half
    c = jnp.where(dim[None, :] < ROPE_DIM, cos[:, idx], 1.0)
    s_lo = jnp.where(dim[None, :] < half, -sin[:, idx], 0.0)
    s_hi = jnp.where((dim[None, :] >= half) & (dim[None, :] < ROPE_DIM), sin[:, idx], 0.0)
    return c.astype(F32), s_lo.astype(F32), s_hi.astype(F32)


def kernel(x, ffn1_norm, ffn1_w1, ffn1_w3, ffn1_w2, mix_norm, w_in, q_norm, k_norm,
           hg_lb_logits, hg_out_norm, w_out, ffn2_norm, ffn2_w1, ffn2_w3, ffn2_w2):
    b, s, d = x.shape
    depth = ffn1_norm.shape[0]
    tm = 512
    cos, s_lo, s_hi = _rope_lane_tables(s)
    head_of = jnp.arange(2 * LANES) // HEAD_DIM
    seg = (head_of[:, None] == head_of[None, :]).astype(BF16)
    for layer in range(depth):
        x1 = _ffn1(x.reshape(b * s, d), ffn1_norm[layer][None, :], ffn1_w1[layer].astype(BF16),
                   ffn1_w3[layer].astype(BF16), ffn1_w2[layer].astype(BF16), tm).reshape(b, s, d)
        (q1, k1, v1, q4, k4, v4, q16, k16, v16, hq, hf, hi, hg) = _in_proj(
            x1, mix_norm[layer][None, :], w_in[layer].astype(BF16), seg,
            jnp.tile(q_norm[layer], ATTN_HEADS)[None, :], jnp.tile(k_norm[layer], ATTN_HEADS)[None, :],
            cos, s_lo, s_hi, tm)
        os_, ms, dens = [], [], []
        for dilation, groups, (q, k, v) in zip(
                DILATIONS, (1, 1, 4),
                ((q1[:, None], k1[:, None], v1[:, None]), (q4, k4, v4), (q16, k16, v16))):
            o, m, den = _attention(q, k, v, dilation, groups)
            os_.append(o)
            ms.append(m)
            dens.append(den)
        rec = _hgrn2(hg_lb_logits, hq, hf, hi, hg, hg_out_norm[layer][None, :], layer)
        x = _out_ffn2(x1, os_, ms, dens, rec, w_out[layer].astype(BF16),
                      ffn2_norm[layer][None, :], ffn2_w1[layer].astype(BF16),
                      ffn2_w3[layer].astype(BF16), ffn2_w2[layer].astype(BF16), tm)
    return x
```

```python
import functools

import jax
import jax.numpy as jnp
from jax import lax
from jax.experimental import pallas as pl
from jax.experimental.pallas import tpu as pltpu

F32 = jnp.float32
BF16 = jnp.bfloat16

EPS = 1e-6
NEG_INF = -1e30
LOG2_E = 1.4426950408889634
HEAD_DIM = 64
ATTN_HEADS = 8
ATTN_WIDTH = ATTN_HEADS * HEAD_DIM
ROPE_DIM = HEAD_DIM // 4
ROPE_THETA = 500000.0
DILATIONS = (1, 4, 16)
HG_HEADS = 4
HG_DIM = 128
HG_CHUNK = 64
HG_WIDTH = HG_HEADS * HG_DIM

LANES = 128
ATTN_BLOCK = 128
PAIRS = ATTN_WIDTH // LANES
ATTN_CHUNK = 4
IN_PROJ_SUBTILES = 2
VMEM_LIMIT = 56 * 1024 * 1024


def _const_spec(shape):
    nd = len(shape)
    return pl.BlockSpec(shape, lambda *_: (0,) * nd, pipeline_mode=pl.Buffered(1))


def _rms_norm_rows(x, w):
    return x * lax.rsqrt(jnp.mean(x * x, axis=-1, keepdims=True) + EPS) * w


def _silu(a):
    return a * (1.0 / (1.0 + jnp.exp(-a)))


def _swiglu_half_step(x, nw, w1_ref, w3_ref, w2_ref):
    h = _rms_norm_rows(x, nw).astype(BF16)
    a = jnp.dot(h, w1_ref[...], preferred_element_type=F32)
    b = jnp.dot(h, w3_ref[...], preferred_element_type=F32)
    g = (_silu(a) * b).astype(BF16)
    y = jnp.dot(g, w2_ref[...], preferred_element_type=F32)
    return x + 0.5 * y


def _split_bf16(x):
    hi = x.astype(BF16)
    return hi, (x - hi.astype(F32)).astype(BF16)


def _ffn1_kernel(x_ref, nw_ref, w1_ref, w3_ref, w2_ref, o_ref):
    o_ref[...] = _swiglu_half_step(x_ref[...], nw_ref[...], w1_ref, w3_ref, w2_ref)


def _ffn1(x2d, nw, w1, w3, w2, tm):
    n, d = x2d.shape
    f = w1.shape[1]
    return pl.pallas_call(
        _ffn1_kernel,
        grid=(n // tm,),
        in_specs=[
            pl.BlockSpec((tm, d), lambda i: (i, 0)),
            _const_spec((1, d)),
            _const_spec((d, f)),
            _const_spec((d, f)),
            _const_spec((f, d)),
        ],
        out_specs=pl.BlockSpec((tm, d), lambda i: (i, 0)),
        out_shape=jax.ShapeDtypeStruct((n, d), F32),
        compiler_params=pltpu.CompilerParams(
            dimension_semantics=("arbitrary",), vmem_limit_bytes=VMEM_LIMIT),
        name="ffn1",
    )(x2d, nw, w1, w3, w2)


def _head_norm_rope(t, seg_ref, w, cos, sin_lo, sin_hi, scale):
    hi, lo = _split_bf16(t * t)
    seg = seg_ref[...]
    sw = seg.shape[0]
    ms = jnp.concatenate(
        [jnp.dot(hi[:, c:c + sw], seg, preferred_element_type=F32)
         + jnp.dot(lo[:, c:c + sw], seg, preferred_element_type=F32)
         for c in range(0, ATTN_WIDTH, sw)], axis=1) * (1.0 / HEAD_DIM)
    y = t * lax.rsqrt(ms + EPS) * w
    outs = []
    for c in range(PAIRS):
        yc = y[:, c * LANES:(c + 1) * LANES]
        from_hi = pltpu.roll(yc, LANES - ROPE_DIM // 2, axis=1)
        from_lo = pltpu.roll(yc, ROPE_DIM // 2, axis=1)
        outs.append((yc * cos + from_hi * sin_lo + from_lo * sin_hi) * scale)
    return outs


def _in_proj_kernel(x_ref, nw_ref, w_ref, seg_ref, qw_ref, kw_ref, cos_ref, slo_ref, shi_ref,
                    q1_ref, k1_ref, v1_ref, q4_ref, k4_ref, v4_ref, q16_ref, k16_ref, v16_ref,
                    hq_ref, hf_ref, hi_ref, hg_ref, slab_scr, res4_scr):
    tm = x_ref.shape[0]
    a = ATTN_WIDTH
    outs = ((q1_ref, q4_ref, q16_ref), (k1_ref, k4_ref, k16_ref), (v1_ref, v4_ref, v16_ref))
    nsub = slab_scr.shape[0]
    ts = tm // nsub
    for sub in range(nsub):
        rows = slice(sub * ts, (sub + 1) * ts)
        h = _rms_norm_rows(x_ref[rows, :], nw_ref[...]).astype(BF16)
        p = jnp.dot(h, w_ref[...], preferred_element_type=F32)
        cos, slo, shi = cos_ref[rows, :], slo_ref[rows, :], shi_ref[rows, :]
        qs = _head_norm_rope(p[:, 0:a], seg_ref, qw_ref[...], cos, slo, shi,
                             LOG2_E * HEAD_DIM ** -0.5)
        ks = _head_norm_rope(p[:, a:2 * a], seg_ref, kw_ref[...], cos, slo, shi, 1.0)
        vs = [p[:, 2 * a + c * LANES:2 * a + (c + 1) * LANES] for c in range(PAIRS)]
        for ti, slabs in enumerate((qs, ks, vs)):
            o1, o4, o16 = outs[ti]
            for c in range(PAIRS):
                lanes = slice(c * LANES, (c + 1) * LANES)
                o1[0, rows, lanes] = slabs[c].astype(BF16)
                sl = ti * PAIRS + c
                slab_scr[sub, sl] = slabs[c]
                for r in range(4):
                    rows4 = slab_scr[sub, sl, pl.ds(r, ts // 4, stride=4), :]
                    o4[0, r, sub * (ts // 4):(sub + 1) * (ts // 4), lanes] = rows4.astype(BF16)
                    res4_scr[sub, sl, r] = rows4
                    for a_ in range(4):
                        rows16 = res4_scr[sub, sl, r, pl.ds(a_, ts // 16, stride=4), :]
                        o16[0, 4 * a_ + r, sub * (ts // 16):(sub + 1) * (ts // 16), lanes] = (
                            rows16.astype(BF16))
        o = 3 * a
        hq_ref[0, rows, :] = p[:, o:o + HG_WIDTH].astype(BF16)
        hf_ref[0, rows, :] = p[:, o + HG_WIDTH:o + 2 * HG_WIDTH]
        hi_ref[0, rows, :] = p[:, o + 2 * HG_WIDTH:o + 3 * HG_WIDTH].astype(BF16)
        hg_ref[0, rows, :] = p[:, o + 3 * HG_WIDTH:o + 4 * HG_WIDTH].astype(BF16)


def _in_proj(x3d, nw, w_in, seg, qw, kw, cos, slo, shi, tm):
    b, s, d = x3d.shape
    cols = w_in.shape[1]
    a = ATTN_WIDTH
    ts = tm // IN_PROJ_SUBTILES
    row = lambda i, j: (i, j, 0)
    pos = lambda i, j: (j, 0)
    out_specs, out_shape = [], []
    for dil in DILATIONS:
        for _ in range(3):
            if dil == 1:
                out_specs.append(pl.BlockSpec((1, tm, a), row))
                out_shape.append(jax.ShapeDtypeStruct((b, s, a), BF16))
            else:
                out_specs.append(pl.BlockSpec((1, dil, tm // dil, a), lambda i, j: (i, 0, j, 0)))
                out_shape.append(jax.ShapeDtypeStruct((b, dil, s // dil, a), BF16))
    for dt in (BF16, F32, BF16, BF16):
        out_specs.append(pl.BlockSpec((1, tm, HG_WIDTH), row))
        out_shape.append(jax.ShapeDtypeStruct((b, s, HG_WIDTH), dt))
    return pl.pallas_call(
        _in_proj_kernel,
        grid=(b, s // tm),
        in_specs=[
            pl.BlockSpec((None, tm, d), row),
            _const_spec((1, d)),
            _const_spec((d, cols)),
            _const_spec(seg.shape),
            _const_spec((1, a)),
            _const_spec((1, a)),
            pl.BlockSpec((tm, LANES), pos),
            pl.BlockSpec((tm, LANES), pos),
            pl.BlockSpec((tm, LANES), pos),
        ],
        out_specs=out_specs,
        out_shape=out_shape,
        scratch_shapes=[pltpu.VMEM((IN_PROJ_SUBTILES, 3 * PAIRS, ts, LANES), F32),
                        pltpu.VMEM((IN_PROJ_SUBTILES, 3 * PAIRS, 4, ts // 4, LANES), F32)],
        compiler_params=pltpu.CompilerParams(
            dimension_semantics=("arbitrary", "arbitrary"), vmem_limit_bytes=VMEM_LIMIT),
        name="in_proj",
    )(x3d, nw, w_in, seg, qw, kw, cos, slo, shi)


def _attn_kernel(q_ref, k_ref, v_ref, o_ref, m_ref, den_ref, p_scr, *, dilation, nblk):
    w = ATTN_BLOCK
    groups = dilation
    lane = lax.broadcasted_iota(jnp.int32, (w, LANES), 1)
    head0 = lane < HEAD_DIM
    qi1 = lax.broadcasted_iota(jnp.int32, (2 * w, w), 0) % w
    kj1 = lax.broadcasted_iota(jnp.int32, (2 * w, w), 1)
    valid_first = kj1 <= qi1
    qi2 = lax.broadcasted_iota(jnp.int32, (2 * w, 2 * w), 0) % w
    kj2 = lax.broadcasted_iota(jnp.int32, (2 * w, 2 * w), 1)
    dist = qi2 + w - kj2
    valid_band = (dist >= 0) & (dist <= w)

    def to_heads_major(acc):
        return acc.T[:ATTN_HEADS, :]

    def score_block(g, qb, r0, rows_k, nk, valid):
        rows_q = pl.ds(r0, w)
        m_acc = jnp.zeros((w, LANES), F32)
        for hp in range(PAIRS):
            lanes = slice(hp * LANES, (hp + 1) * LANES)
            q2 = q_ref[0, g, rows_q, lanes]
            zero = jnp.zeros_like(q2)
            qq = jnp.concatenate([jnp.where(head0, q2, zero), jnp.where(head0, zero, q2)], axis=0)
            s = lax.dot_general(qq, k_ref[0, g, rows_k, lanes], (((1,), (1,)), ((), ())),
                                preferred_element_type=F32)
            s = jnp.where(valid, s, NEG_INF)
            m = jnp.max(s, axis=1, keepdims=True)
            p_scr[g * nblk + qb, hp, :, 0:nk] = jnp.exp2(s - m).astype(BF16)
            for h in range(2):
                m_acc = jnp.where(lane == 2 * hp + h, m[h * w:(h + 1) * w], m_acc)
        m_ref[0, g, qb] = to_heads_major(m_acc)

    def value_block(g, qb, r0, rows_k, nk):
        d_acc = jnp.zeros((w, LANES), F32)
        ones = jnp.ones((nk, LANES), BF16)
        for hp in range(PAIRS):
            lanes = slice(hp * LANES, (hp + 1) * LANES)
            vx = jnp.concatenate([v_ref[0, g, rows_k, lanes], ones], axis=1)
            r = jnp.dot(p_scr[g * nblk + qb, hp, :, 0:nk], vx, preferred_element_type=F32)
            o = jnp.where(head0, r[:w, :LANES], r[w:, :LANES])
            if dilation == 1:
                o_ref[0, hp, pl.ds(r0, w), :] = o
            else:
                tok0 = r0 * dilation + g
                o_ref[0, hp, pl.ds(tok0, w, stride=dilation), :] = o
            for h in range(2):
                d_acc = jnp.where(lane == 2 * hp + h, r[h * w:(h + 1) * w, LANES:], d_acc)
        den_ref[0, g, qb] = to_heads_major(d_acc)

    def run_chunk(blocks):
        args = []
        for g, qb in blocks:
            if isinstance(qb, int) and qb == 0:
                args.append((g, 0, 0, pl.ds(0, w), w, valid_first))
            else:
                r0 = qb * w if isinstance(qb, int) else pl.multiple_of(qb * w, w)
                args.append((g, qb, r0, pl.ds(r0 - w, 2 * w), 2 * w, valid_band))
        for g, qb, r0, rows_k, nk, valid in args:
            score_block(g, qb, r0, rows_k, nk, valid)
        for g, qb, r0, rows_k, nk, _ in args:
            value_block(g, qb, r0, rows_k, nk)

    def loop_chunks(lo, hi, blocks_of):
        def body(i, carry):
            run_chunk(blocks_of(i))
            return carry
        lax.fori_loop(lo, hi, body, 0)

    c = ATTN_CHUNK
    if nblk == 1:
        loop_chunks(0, groups // c, lambda i: [(i * c + j, 0) for j in range(c)])
    elif nblk == c:
        loop_chunks(0, groups, lambda g: [(g, qb) for qb in range(c)])
    else:
        run_chunk([(0, qb) for qb in range(c)])
        loop_chunks(1, nblk // c, lambda i: [(0, i * c + j) for j in range(c)])


def _attention(q, k, v):
    b, dilation, l, a = q.shape
    s = l * dilation
    nblk = l // ATTN_BLOCK
    spec = pl.BlockSpec((1, dilation, l, a), lambda i: (i, 0, 0, 0))
    stat_spec = pl.BlockSpec((1, dilation, nblk, ATTN_HEADS, ATTN_BLOCK), lambda i: (i, 0, 0, 0, 0))
    stat_shape = jax.ShapeDtypeStruct((b, dilation, nblk, ATTN_HEADS, ATTN_BLOCK), F32)
    o, m, den = pl.pallas_call(
        functools.partial(_attn_kernel, dilation=dilation, nblk=nblk),
        grid=(b,),
        in_specs=[spec, spec, spec],
        out_specs=[pl.BlockSpec((1, PAIRS, s, LANES), lambda i: (i, 0, 0, 0)), stat_spec, stat_spec],
        out_shape=[jax.ShapeDtypeStruct((b, PAIRS, s, LANES), F32), stat_shape, stat_shape],
        scratch_shapes=[pltpu.VMEM((dilation * nblk, PAIRS, 2 * ATTN_BLOCK, 2 * ATTN_BLOCK), BF16)],
        compiler_params=pltpu.CompilerParams(
            dimension_semantics=("arbitrary",), vmem_limit_bytes=VMEM_LIMIT),
        name=f"attn_d{dilation}",
    )(q, k, v)
    nat = lambda t: t.transpose(0, 2, 4, 1, 3).reshape(b, s, ATTN_HEADS)
    return o, nat(m), nat(den)


def _hgrn_kernel(lbl_ref, q_ref, f_ref, i_ref, g_ref, nw_ref, o_ref,
                 qd_scr, kd_scr, kl_scr, att_scr, cs_scr, dec_scr, st_scr, *, seq, layer):
    c = HG_CHUNK
    t2 = 2 * c
    ntile = seq // t2
    lg = lbl_ref[...]
    e = jnp.exp(lg - jnp.max(lg, axis=0, keepdims=True))
    lb = jnp.sum(e[0:layer + 1, :], axis=0, keepdims=True) / jnp.sum(e, axis=0, keepdims=True)

    row = lax.broadcasted_iota(jnp.int32, (t2, HG_DIM), 0)
    rin = row % c
    col = lax.broadcasted_iota(jnp.int32, (t2, t2), 1)
    rr = lax.broadcasted_iota(jnp.int32, (t2, t2), 0)
    tril = ((rr // c) == (col // c)) & ((col % c) <= (rr % c))

    def decay_step(t, carry):
        r0 = pl.multiple_of(t * t2, t2)
        rows = pl.ds(r0, t2)
        f = lb + (1.0 - lb) * (1.0 / (1.0 + jnp.exp(-f_ref[0, rows, :])))
        gl = jnp.log(f)
        kk = 1.0 - f
        for sft in (1, 2, 4, 8, 16, 32):
            gl = gl + jnp.where(rin >= sft, pltpu.roll(gl, sft, axis=0), 0.0)
        g_last = jnp.where(row < c, gl[c - 1:c, :], gl[t2 - 1:t2, :])
        qd_scr[rows, :] = (q_ref[0, rows, :].astype(F32) * jnp.exp(gl)).astype(BF16)
        kd_scr[rows, :] = (kk * jnp.exp(-gl)).astype(BF16)
        kl_scr[rows, :] = (kk * jnp.exp(g_last - gl)).astype(BF16)
        for j in range(2):
            dec_scr[2 * t + j] = jnp.exp(gl[(j + 1) * c - 1:(j + 1) * c, :])
        return carry

    lax.fori_loop(0, ntile, decay_step, 0, unroll=4)

    first_chunk = row < c

    def score_step(t, carry):
        r0 = pl.multiple_of(t * t2, t2)
        rows = pl.ds(r0, t2)
        att = lax.dot_general(qd_scr[rows, :], kd_scr[rows, :], (((1,), (1,)), ((), ())),
                              preferred_element_type=F32)
        att_scr[rows, :] = jnp.where(tril, att, 0.0).astype(BF16)
        kl = kl_scr[rows, :]
        zero = jnp.zeros_like(kl)
        kl2 = jnp.concatenate([jnp.where(first_chunk, kl, zero), jnp.where(first_chunk, zero, kl)],
                              axis=1)
        cs2 = lax.dot_general(i_ref[0, rows, :], kl2, (((0,), (0,)), ((), ())),
                              preferred_element_type=F32)
        cs_scr[2 * t] = cs2[:, :HG_DIM]
        cs_scr[2 * t + 1] = cs2[:, HG_DIM:]
        return carry

    lax.fori_loop(0, ntile, score_step, 0, unroll=4)

    def scan_step(ch, state_t):
        st_scr[pl.ds(pl.multiple_of(ch * HG_DIM, HG_DIM), HG_DIM), :] = state_t.astype(BF16)
        return state_t * dec_scr[ch] + cs_scr[ch]

    lax.fori_loop(0, seq // c, scan_step, jnp.zeros((HG_DIM, HG_DIM), F32), unroll=4)

    nw = nw_ref[...]

    def out_step(t, carry):
        r0 = pl.multiple_of(t * t2, t2)
        rows = pl.ds(r0, t2)
        intra = jnp.dot(att_scr[rows, :], i_ref[0, rows, :], preferred_element_type=F32)
        states = st_scr[pl.ds(pl.multiple_of(t * 2 * HG_DIM, 2 * HG_DIM), 2 * HG_DIM), :]
        inter2 = lax.dot_general(qd_scr[rows, :], states, (((1,), (1,)), ((), ())),
                                 preferred_element_type=F32)
        rec = intra + jnp.where(first_chunk, inter2[:, :HG_DIM], inter2[:, HG_DIM:])
        gate = g_ref[0, rows, :].astype(F32)
        o_ref[0, rows, :] = (_rms_norm_rows(rec, nw) * _silu(gate)).astype(BF16)
        return carry

    lax.fori_loop(0, ntile, out_step, 0, unroll=4)


def _hgrn2(lb_logits, hq, hf, hi, hg, nw, layer):
    b, s, _ = hq.shape
    nl = lb_logits.shape[0]
    nchunk = s // HG_CHUNK
    spec = pl.BlockSpec((1, s, HG_DIM), lambda i, h: (i, 0, h))
    return pl.pallas_call(
        functools.partial(_hgrn_kernel, seq=s, layer=layer),
        grid=(b, HG_HEADS),
        in_specs=[pl.BlockSpec((nl, HG_DIM), lambda i, h: (0, h)),
                  spec, spec, spec, spec,
                  pl.BlockSpec((1, HG_DIM), lambda i, h: (0, 0))],
        out_specs=spec,
        out_shape=jax.ShapeDtypeStruct((b, s, HG_WIDTH), BF16),
        scratch_shapes=[pltpu.VMEM((s, HG_DIM), BF16),
                        pltpu.VMEM((s, HG_DIM), BF16),
                        pltpu.VMEM((s, HG_DIM), BF16),
                        pltpu.VMEM((s, 2 * HG_CHUNK), BF16),
                        pltpu.VMEM((nchunk, HG_DIM, HG_DIM), F32),
                        pltpu.VMEM((nchunk, 1, HG_DIM), F32),
                        pltpu.VMEM((nchunk * HG_DIM, HG_DIM), BF16)],
        compiler_params=pltpu.CompilerParams(
            dimension_semantics=("arbitrary", "arbitrary"), vmem_limit_bytes=VMEM_LIMIT),
        name="hgrn2",
    )(lb_logits, hq, hf, hi, hg, nw)


def _out_ffn2_kernel(x_ref, o1_ref, o2_ref, o3_ref, m1_ref, m2_ref, m3_ref,
                     d1_ref, d2_ref, d3_ref, rec_ref,
                     wo_ref, nw_ref, w1_ref, w3_ref, w2_ref, out_ref):
    o_refs = (o1_ref, o2_ref, o3_ref)
    ms = [r[0] for r in (m1_ref, m2_ref, m3_ref)]
    dens = [r[0] for r in (d1_ref, d2_ref, d3_ref)]
    mx = jnp.maximum(jnp.maximum(ms[0], ms[1]), ms[2])
    es = [jnp.exp2(m - mx) for m in ms]
    inv = 1.0 / (es[0] * dens[0] + es[1] * dens[1] + es[2] * dens[2])
    halves = [h for e in es for h in _split_bf16(e * inv)]
    stacked = jnp.concatenate(halves, axis=1)
    kdim = stacked.shape[1]
    src = lax.broadcasted_iota(jnp.int32, (kdim, len(es) * ATTN_WIDTH), 0)
    dst = lax.broadcasted_iota(jnp.int32, (kdim, len(es) * ATTN_WIDTH), 1)
    expand = ((src // (2 * ATTN_HEADS) == dst // ATTN_WIDTH)
              & (src % ATTN_HEADS == (dst % ATTN_WIDTH) // HEAD_DIM)).astype(BF16)
    wide = jnp.dot(stacked, expand, preferred_element_type=F32)
    parts = []
    for hp in range(PAIRS):
        acc = None
        for p in range(len(es)):
            lo_lane = p * ATTN_WIDTH + hp * LANES
            term = wide[:, lo_lane:lo_lane + LANES] * o_refs[p][0, hp]
            acc = term if acc is None else acc + term
        parts.append(acc.astype(BF16))
    mixed = jnp.concatenate(parts + [rec_ref[0]], axis=1)
    x2 = x_ref[0] + jnp.dot(mixed, wo_ref[...], preferred_element_type=F32)
    out_ref[0] = _swiglu_half_step(x2, nw_ref[...], w1_ref, w3_ref, w2_ref)


def _out_ffn2(x3d, os_, ms, dens, rec, wo, nw, w1, w3, w2, tm):
    b, s, d = x3d.shape
    f = w1.shape[1]
    a = ATTN_WIDTH
    row = lambda i, j: (i, j, 0)
    return pl.pallas_call(
        _out_ffn2_kernel,
        grid=(b, s // tm),
        in_specs=[pl.BlockSpec((1, tm, d), row)]
                 + [pl.BlockSpec((1, PAIRS, tm, LANES), lambda i, j: (i, 0, j, 0))] * 3
                 + [pl.BlockSpec((1, tm, ATTN_HEADS), row)] * 6
                 + [pl.BlockSpec((1, tm, HG_WIDTH), row),
                    _const_spec((a + HG_WIDTH, d)),
                    _const_spec((1, d)),
                    _const_spec((d, f)),
                    _const_spec((d, f)),
                    _const_spec((f, d))],
        out_specs=pl.BlockSpec((1, tm, d), row),
        out_shape=jax.ShapeDtypeStruct((b, s, d), F32),
        compiler_params=pltpu.CompilerParams(
            dimension_semantics=("arbitrary", "arbitrary"), vmem_limit_bytes=VMEM_LIMIT),
        name="out_ffn2",
    )(x3d, *os_, *ms, *dens, rec, wo, nw, w1, w3, w2)


def _rope_lane_tables(s):
    half = ROPE_DIM // 2
    inv = ROPE_THETA ** (-jnp.arange(0, ROPE_DIM, 2, dtype=F32) / ROPE_DIM)
    ang = jnp.arange(s, dtype=F32)[:, None] * inv[None, :]
    cos, sin = jnp.cos(ang), jnp.sin(ang)
    dim = jnp.arange(LANES) % HEAD_DIM
    idx = dim % half
    c = jnp.where(dim[None, :] < ROPE_DIM, cos[:, idx], 1.0)
    s_lo = jnp.where(dim[None, :] < half, -sin[:, idx], 0.0)
    s_hi = jnp.where((dim[None, :] >= half) & (dim[None, :] < ROPE_DIM), sin[:, idx], 0.0)
    return c.astype(F32), s_lo.astype(F32), s_hi.astype(F32)


def kernel(x, ffn1_norm, ffn1_w1, ffn1_w3, ffn1_w2, mix_norm, w_in, q_norm, k_norm,
           hg_lb_logits, hg_out_norm, w_out, ffn2_norm, ffn2_w1, ffn2_w3, ffn2_w2):
    b, s, d = x.shape
    depth = ffn1_norm.shape[0]
    tm = 512
    cos, s_lo, s_hi = _rope_lane_tables(s)
    head_of = jnp.arange(2 * LANES) // HEAD_DIM
    seg = (head_of[:, None] == head_of[None, :]).astype(BF16)
    for layer in range(depth):
        x1 = _ffn1(x.reshape(b * s, d), ffn1_norm[layer][None, :], ffn1_w1[layer].astype(BF16),
                   ffn1_w3[layer].astype(BF16), ffn1_w2[layer].astype(BF16), tm).reshape(b, s, d)
        (q1, k1, v1, q4, k4, v4, q16, k16, v16, hq, hf, hi, hg) = _in_proj(
            x1, mix_norm[layer][None, :], w_in[layer].astype(BF16), seg,
            jnp.tile(q_norm[layer], ATTN_HEADS)[None, :], jnp.tile(k_norm[layer], ATTN_HEADS)[None, :],
            cos, s_lo, s_hi, tm)
        os_, ms, dens = [], [], []
        for q, k, v in ((q1[:, None], k1[:, None], v1[:, None]), (q4, k4, v4), (q16, k16, v16)):
            o, m, den = _attention(q, k, v)
            os_.append(o)
            ms.append(m)
            dens.append(den)
        rec = _hgrn2(hg_lb_logits, hq, hf, hi, hg, hg_out_norm[layer][None, :], layer)
        x = _out_ffn2(x1, os_, ms, dens, rec, w_out[layer].astype(BF16),
                      ffn2_norm[layer][None, :], ffn2_w1[layer].astype(BF16),
                      ffn2_w3[layer].astype(BF16), ffn2_w2[layer].astype(BF16), tm)
    return x
```

```python
import functools

import jax
import jax.numpy as jnp
from jax import lax
from jax.experimental import pallas as pl
from jax.experimental.pallas import tpu as pltpu

F32 = jnp.float32
BF16 = jnp.bfloat16

EPS = 1e-6
NEG_INF = -1e30
LOG2_E = 1.4426950408889634
HEAD_DIM = 64
ATTN_HEADS = 8
ATTN_WIDTH = ATTN_HEADS * HEAD_DIM
ROPE_DIM = HEAD_DIM // 4
ROPE_THETA = 500000.0
DILATIONS = (1, 4, 16)
HG_HEADS = 4
HG_DIM = 128
HG_CHUNK = 64
HG_WIDTH = HG_HEADS * HG_DIM

LANES = 128
ATTN_BLOCK = 128
PAIRS = ATTN_WIDTH // LANES
ATTN_CHUNK = 4
ATTN_SLABS = PAIRS + 2
IN_PROJ_SUBTILES = 2
VMEM_LIMIT = 56 * 1024 * 1024


def _const_spec(shape):
    nd = len(shape)
    return pl.BlockSpec(shape, lambda *_: (0,) * nd, pipeline_mode=pl.Buffered(1))


def _rms_norm_rows(x, w):
    return x * lax.rsqrt(jnp.mean(x * x, axis=-1, keepdims=True) + EPS) * w


def _silu(a):
    return a * (1.0 / (1.0 + jnp.exp(-a)))


def _swiglu_half_step(x, nw, w1_ref, w3_ref, w2_ref):
    h = _rms_norm_rows(x, nw).astype(BF16)
    a = jnp.dot(h, w1_ref[...], preferred_element_type=F32)
    b = jnp.dot(h, w3_ref[...], preferred_element_type=F32)
    g = (_silu(a) * b).astype(BF16)
    y = jnp.dot(g, w2_ref[...], preferred_element_type=F32)
    return x + 0.5 * y


def _split_bf16(x):
    hi = x.astype(BF16)
    return hi, (x - hi.astype(F32)).astype(BF16)


def _ffn1_kernel(x_ref, nw_ref, w1_ref, w3_ref, w2_ref, o_ref):
    o_ref[...] = _swiglu_half_step(x_ref[...], nw_ref[...], w1_ref, w3_ref, w2_ref)


def _ffn1(x2d, nw, w1, w3, w2, tm):
    n, d = x2d.shape
    f = w1.shape[1]
    return pl.pallas_call(
        _ffn1_kernel,
        grid=(n // tm,),
        in_specs=[
            pl.BlockSpec((tm, d), lambda i: (i, 0)),
            _const_spec((1, d)),
            _const_spec((d, f)),
            _const_spec((d, f)),
            _const_spec((f, d)),
        ],
        out_specs=pl.BlockSpec((tm, d), lambda i: (i, 0)),
        out_shape=jax.ShapeDtypeStruct((n, d), F32),
        compiler_params=pltpu.CompilerParams(
            dimension_semantics=("arbitrary",), vmem_limit_bytes=VMEM_LIMIT),
        name="ffn1",
    )(x2d, nw, w1, w3, w2)


def _head_norm_rope(t, seg_ref, w, cos, sin_lo, sin_hi, scale):
    sq = (t * t).astype(BF16)
    seg = seg_ref[...]
    sw = seg.shape[0]
    ms = jnp.concatenate(
        [jnp.dot(sq[:, c:c + sw], seg, preferred_element_type=F32)
         for c in range(0, ATTN_WIDTH, sw)], axis=1) * (1.0 / HEAD_DIM)
    y = t * lax.rsqrt(ms + EPS) * w
    outs = []
    for c in range(PAIRS):
        yc = y[:, c * LANES:(c + 1) * LANES]
        from_hi = pltpu.roll(yc, LANES - ROPE_DIM // 2, axis=1)
        from_lo = pltpu.roll(yc, ROPE_DIM // 2, axis=1)
        outs.append((yc * cos + from_hi * sin_lo + from_lo * sin_hi) * scale)
    return outs


def _in_proj_kernel(x_ref, nw_ref, w_ref, seg_ref, qw_ref, kw_ref, cos_ref, slo_ref, shi_ref,
                    q1_ref, k1_ref, v1_ref, q4_ref, k4_ref, v4_ref, q16_ref, k16_ref, v16_ref,
                    hq_ref, hf_ref, hi_ref, hg_ref, slab_scr, res4_scr):
    tm = x_ref.shape[0]
    a = ATTN_WIDTH
    outs = ((q1_ref, q4_ref, q16_ref), (k1_ref, k4_ref, k16_ref), (v1_ref, v4_ref, v16_ref))
    nsub = slab_scr.shape[0]
    ts = tm // nsub
    for sub in range(nsub):
        rows = slice(sub * ts, (sub + 1) * ts)
        h = _rms_norm_rows(x_ref[rows, :], nw_ref[...]).astype(BF16)
        p = jnp.dot(h, w_ref[...], preferred_element_type=F32)
        cos, slo, shi = cos_ref[rows, :], slo_ref[rows, :], shi_ref[rows, :]
        qs = _head_norm_rope(p[:, 0:a], seg_ref, qw_ref[...], cos, slo, shi,
                             LOG2_E * HEAD_DIM ** -0.5)
        ks = _head_norm_rope(p[:, a:2 * a], seg_ref, kw_ref[...], cos, slo, shi, 1.0)
        vs = [p[:, 2 * a + c * LANES:2 * a + (c + 1) * LANES] for c in range(PAIRS)]
        for ti, slabs in enumerate((qs, ks, vs)):
            o1, o4, o16 = outs[ti]
            for c in range(PAIRS):
                lanes = slice(c * LANES, (c + 1) * LANES)
                o1[0, rows, lanes] = slabs[c].astype(BF16)
                sl = ti * PAIRS + c
                slab_scr[sub, sl] = slabs[c]
                for r in range(4):
                    rows4 = slab_scr[sub, sl, pl.ds(r, ts // 4, stride=4), :]
                    o4[0, r, sub * (ts // 4):(sub + 1) * (ts // 4), lanes] = rows4.astype(BF16)
                    res4_scr[sub, sl, r] = rows4
                    for a_ in range(4):
                        rows16 = res4_scr[sub, sl, r, pl.ds(a_, ts // 16, stride=4), :]
                        o16[0, 4 * a_ + r, sub * (ts // 16):(sub + 1) * (ts // 16), lanes] = (
                            rows16.astype(BF16))
        o = 3 * a
        hq_ref[0, rows, :] = p[:, o:o + HG_WIDTH].astype(BF16)
        hf_ref[0, rows, :] = p[:, o + HG_WIDTH:o + 2 * HG_WIDTH]
        hi_ref[0, rows, :] = p[:, o + 2 * HG_WIDTH:o + 3 * HG_WIDTH].astype(BF16)
        hg_ref[0, rows, :] = p[:, o + 3 * HG_WIDTH:o + 4 * HG_WIDTH].astype(BF16)


def _in_proj(x3d, nw, w_in, seg, qw, kw, cos, slo, shi, tm):
    b, s, d = x3d.shape
    cols = w_in.shape[1]
    a = ATTN_WIDTH
    ts = tm // IN_PROJ_SUBTILES
    row = lambda i, j: (i, j, 0)
    pos = lambda i, j: (j, 0)
    out_specs, out_shape = [], []
    for dil in DILATIONS:
        for _ in range(3):
            if dil == 1:
                out_specs.append(pl.BlockSpec((1, tm, a), row))
                out_shape.append(jax.ShapeDtypeStruct((b, s, a), BF16))
            else:
                out_specs.append(pl.BlockSpec((1, dil, tm // dil, a), lambda i, j: (i, 0, j, 0)))
                out_shape.append(jax.ShapeDtypeStruct((b, dil, s // dil, a), BF16))
    for dt in (BF16, F32, BF16, BF16):
        out_specs.append(pl.BlockSpec((1, tm, HG_WIDTH), row))
        out_shape.append(jax.ShapeDtypeStruct((b, s, HG_WIDTH), dt))
    return pl.pallas_call(
        _in_proj_kernel,
        grid=(b, s // tm),
        in_specs=[
            pl.BlockSpec((None, tm, d), row),
            _const_spec((1, d)),
            _const_spec((d, cols)),
            _const_spec(seg.shape),
            _const_spec((1, a)),
            _const_spec((1, a)),
            pl.BlockSpec((tm, LANES), pos),
            pl.BlockSpec((tm, LANES), pos),
            pl.BlockSpec((tm, LANES), pos),
        ],
        out_specs=out_specs,
        out_shape=out_shape,
        scratch_shapes=[pltpu.VMEM((IN_PROJ_SUBTILES, 3 * PAIRS, ts, LANES), F32),
                        pltpu.VMEM((IN_PROJ_SUBTILES, 3 * PAIRS, 4, ts // 4, LANES), F32)],
        compiler_params=pltpu.CompilerParams(
            dimension_semantics=("arbitrary", "arbitrary"), vmem_limit_bytes=VMEM_LIMIT),
        name="in_proj",
    )(x3d, nw, w_in, seg, qw, kw, cos, slo, shi)


def _attn_kernel(q_ref, k_ref, v_ref, o_ref, p_scr, *, dilation, nblk):
    w = ATTN_BLOCK
    groups = dilation
    lane = lax.broadcasted_iota(jnp.int32, (w, LANES), 1)
    head0 = lane < HEAD_DIM
    qi1 = lax.broadcasted_iota(jnp.int32, (2 * w, w), 0) % w
    kj1 = lax.broadcasted_iota(jnp.int32, (2 * w, w), 1)
    valid_first = kj1 <= qi1
    qi2 = lax.broadcasted_iota(jnp.int32, (2 * w, 2 * w), 0) % w
    kj2 = lax.broadcasted_iota(jnp.int32, (2 * w, 2 * w), 1)
    dist = qi2 + w - kj2
    valid_band = (dist >= 0) & (dist <= w)

    def store_rows(slab, r0, g, val):
        if dilation == 1:
            o_ref[0, slab, pl.ds(r0, w), :] = val
        else:
            o_ref[0, slab, pl.ds(r0 * dilation + g, w, stride=dilation), :] = val

    def score_block(g, qb, r0, rows_k, nk, valid):
        rows_q = pl.ds(r0, w)
        m_acc = jnp.zeros((w, LANES), F32)
        for hp in range(PAIRS):
            lanes = slice(hp * LANES, (hp + 1) * LANES)
            q2 = q_ref[0, g, rows_q, lanes]
            zero = jnp.zeros_like(q2)
            qq = jnp.concatenate([jnp.where(head0, q2, zero), jnp.where(head0, zero, q2)], axis=0)
            s = lax.dot_general(qq, k_ref[0, g, rows_k, lanes], (((1,), (1,)), ((), ())),
                                preferred_element_type=F32)
            s = jnp.where(valid, s, NEG_INF)
            m = jnp.max(s, axis=1, keepdims=True)
            p_scr[g * nblk + qb, hp, :, 0:nk] = jnp.exp2(s - m).astype(BF16)
            for h in range(2):
                m_acc = jnp.where(lane == 2 * hp + h, m[h * w:(h + 1) * w], m_acc)
        store_rows(PAIRS, r0, g, m_acc)

    def value_block(g, qb, r0, rows_k, nk):
        d_acc = jnp.zeros((w, LANES), F32)
        ones = jnp.ones((nk, LANES), BF16)
        for hp in range(PAIRS):
            lanes = slice(hp * LANES, (hp + 1) * LANES)
            vx = jnp.concatenate([v_ref[0, g, rows_k, lanes], ones], axis=1)
            r = jnp.dot(p_scr[g * nblk + qb, hp, :, 0:nk], vx, preferred_element_type=F32)
            store_rows(hp, r0, g, jnp.where(head0, r[:w, :LANES], r[w:, :LANES]))
            for h in range(2):
                d_acc = jnp.where(lane == 2 * hp + h, r[h * w:(h + 1) * w, LANES:], d_acc)
        store_rows(PAIRS + 1, r0, g, d_acc)

    def run_chunk(blocks):
        args = []
        for g, qb in blocks:
            if isinstance(qb, int) and qb == 0:
                args.append((g, 0, 0, pl.ds(0, w), w, valid_first))
            else:
                r0 = qb * w if isinstance(qb, int) else pl.multiple_of(qb * w, w)
                args.append((g, qb, r0, pl.ds(r0 - w, 2 * w), 2 * w, valid_band))
        for g, qb, r0, rows_k, nk, valid in args:
            score_block(g, qb, r0, rows_k, nk, valid)
        for g, qb, r0, rows_k, nk, _ in args:
            value_block(g, qb, r0, rows_k, nk)

    def loop_chunks(lo, hi, blocks_of):
        def body(i, carry):
            run_chunk(blocks_of(i))
            return carry
        lax.fori_loop(lo, hi, body, 0)

    c = ATTN_CHUNK
    if nblk == 1:
        loop_chunks(0, groups // c, lambda i: [(i * c + j, 0) for j in range(c)])
    elif nblk == c:
        loop_chunks(0, groups, lambda g: [(g, qb) for qb in range(c)])
    else:
        run_chunk([(0, qb) for qb in range(c)])
        loop_chunks(1, nblk // c, lambda i: [(0, i * c + j) for j in range(c)])


def _attention(q, k, v):
    b, dilation, l, a = q.shape
    s = l * dilation
    nblk = l // ATTN_BLOCK
    spec = pl.BlockSpec((1, dilation, l, a), lambda i: (i, 0, 0, 0))
    return pl.pallas_call(
        functools.partial(_attn_kernel, dilation=dilation, nblk=nblk),
        grid=(b,),
        in_specs=[spec, spec, spec],
        out_specs=pl.BlockSpec((1, ATTN_SLABS, s, LANES), lambda i: (i, 0, 0, 0)),
        out_shape=jax.ShapeDtypeStruct((b, ATTN_SLABS, s, LANES), F32),
        scratch_shapes=[pltpu.VMEM((dilation * nblk, PAIRS, 2 * ATTN_BLOCK, 2 * ATTN_BLOCK), BF16)],
        compiler_params=pltpu.CompilerParams(
            dimension_semantics=("arbitrary",), vmem_limit_bytes=VMEM_LIMIT),
        name=f"attn_d{dilation}",
    )(q, k, v)


def _hgrn_kernel(lbl_ref, q_ref, f_ref, i_ref, g_ref, nw_ref, o_ref,
                 qd_scr, att_scr, cs_scr, dec_scr, st_scr, *, seq, layer):
    c = HG_CHUNK
    t2 = 2 * c
    ntile = seq // t2
    lg = lbl_ref[...]
    e = jnp.exp(lg - jnp.max(lg, axis=0, keepdims=True))
    lb = jnp.sum(e[0:layer + 1, :], axis=0, keepdims=True) / jnp.sum(e, axis=0, keepdims=True)

    row = lax.broadcasted_iota(jnp.int32, (t2, HG_DIM), 0)
    rin = row % c
    col = lax.broadcasted_iota(jnp.int32, (t2, t2), 1)
    rr = lax.broadcasted_iota(jnp.int32, (t2, t2), 0)
    tril = ((rr // c) == (col // c)) & ((col % c) <= (rr % c))

    first_chunk = row < c

    def score_step(t, carry):
        r0 = pl.multiple_of(t * t2, t2)
        rows = pl.ds(r0, t2)
        f = lb + (1.0 - lb) * (1.0 / (1.0 + jnp.exp(-f_ref[0, rows, :])))
        gl = jnp.log(f)
        kk = 1.0 - f
        for sft in (1, 2, 4, 8, 16, 32):
            gl = gl + jnp.where(rin >= sft, pltpu.roll(gl, sft, axis=0), 0.0)
        g_last = jnp.where(first_chunk, gl[c - 1:c, :], gl[t2 - 1:t2, :])
        qd = (q_ref[0, rows, :].astype(F32) * jnp.exp(gl)).astype(BF16)
        kd = (kk * jnp.exp(-gl)).astype(BF16)
        kl = (kk * jnp.exp(g_last - gl)).astype(BF16)
        qd_scr[rows, :] = qd
        for j in range(2):
            dec_scr[2 * t + j] = jnp.exp(gl[(j + 1) * c - 1:(j + 1) * c, :])
        att = lax.dot_general(qd, kd, (((1,), (1,)), ((), ())), preferred_element_type=F32)
        att_scr[rows, :] = jnp.where(tril, att, 0.0).astype(BF16)
        zero = jnp.zeros_like(kl)
        kl2 = jnp.concatenate([jnp.where(first_chunk, kl, zero), jnp.where(first_chunk, zero, kl)],
                              axis=1)
        cs2 = lax.dot_general(i_ref[0, rows, :], kl2, (((0,), (0,)), ((), ())),
                              preferred_element_type=F32)
        cs_scr[2 * t] = cs2[:, :HG_DIM]
        cs_scr[2 * t + 1] = cs2[:, HG_DIM:]
        return carry

    lax.fori_loop(0, ntile, score_step, 0, unroll=8)

    def scan_step(ch, state_t):
        st_scr[pl.ds(pl.multiple_of(ch * HG_DIM, HG_DIM), HG_DIM), :] = state_t.astype(BF16)
        return state_t * dec_scr[ch] + cs_scr[ch]

    lax.fori_loop(0, seq // c, scan_step, jnp.zeros((HG_DIM, HG_DIM), F32), unroll=4)

    nw = nw_ref[...]

    def out_step(t, carry):
        r0 = pl.multiple_of(t * t2, t2)
        rows = pl.ds(r0, t2)
        intra = jnp.dot(att_scr[rows, :], i_ref[0, rows, :], preferred_element_type=F32)
        states = st_scr[pl.ds(pl.multiple_of(t * 2 * HG_DIM, 2 * HG_DIM), 2 * HG_DIM), :]
        inter2 = lax.dot_general(qd_scr[rows, :], states, (((1,), (1,)), ((), ())),
                                 preferred_element_type=F32)
        rec = intra + jnp.where(first_chunk, inter2[:, :HG_DIM], inter2[:, HG_DIM:])
        gate = g_ref[0, rows, :].astype(F32)
        o_ref[0, rows, :] = (_rms_norm_rows(rec, nw) * _silu(gate)).astype(BF16)
        return carry

    lax.fori_loop(0, ntile, out_step, 0, unroll=8)


def _hgrn2(lb_logits, hq, hf, hi, hg, nw, layer):
    b, s, _ = hq.shape
    nl = lb_logits.shape[0]
    nchunk = s // HG_CHUNK
    spec = pl.BlockSpec((1, s, HG_DIM), lambda i, h: (i, 0, h))
    return pl.pallas_call(
        functools.partial(_hgrn_kernel, seq=s, layer=layer),
        grid=(b, HG_HEADS),
        in_specs=[pl.BlockSpec((nl, HG_DIM), lambda i, h: (0, h)),
                  spec, spec, spec, spec,
                  pl.BlockSpec((1, HG_DIM), lambda i, h: (0, 0))],
        out_specs=spec,
        out_shape=jax.ShapeDtypeStruct((b, s, HG_WIDTH), BF16),
        scratch_shapes=[pltpu.VMEM((s, HG_DIM), BF16),
                        pltpu.VMEM((s, 2 * HG_CHUNK), BF16),
                        pltpu.VMEM((nchunk, HG_DIM, HG_DIM), F32),
                        pltpu.VMEM((nchunk, 1, HG_DIM), F32),
                        pltpu.VMEM((nchunk * HG_DIM, HG_DIM), BF16)],
        compiler_params=pltpu.CompilerParams(
            dimension_semantics=("arbitrary", "arbitrary"), vmem_limit_bytes=VMEM_LIMIT),
        name="hgrn2",
    )(lb_logits, hq, hf, hi, hg, nw)


def _out_ffn2_kernel(x_ref, o1_ref, o2_ref, o3_ref, rec_ref,
                     wo_ref, nw_ref, w1_ref, w3_ref, w2_ref, out_ref):
    o_refs = (o1_ref, o2_ref, o3_ref)
    ms = [r[0, PAIRS, :, 0:ATTN_HEADS] for r in o_refs]
    dens = [r[0, PAIRS + 1, :, 0:ATTN_HEADS] for r in o_refs]
    mx = jnp.maximum(jnp.maximum(ms[0], ms[1]), ms[2])
    es = [jnp.exp2(m - mx) for m in ms]
    inv = 1.0 / (es[0] * dens[0] + es[1] * dens[1] + es[2] * dens[2])
    halves = [h for e in es for h in _split_bf16(e * inv)]
    stacked = jnp.concatenate(halves, axis=1)
    kdim = stacked.shape[1]
    src = lax.broadcasted_iota(jnp.int32, (kdim, len(es) * ATTN_WIDTH), 0)
    dst = lax.broadcasted_iota(jnp.int32, (kdim, len(es) * ATTN_WIDTH), 1)
    expand = ((src // (2 * ATTN_HEADS) == dst // ATTN_WIDTH)
              & (src % ATTN_HEADS == (dst % ATTN_WIDTH) // HEAD_DIM)).astype(BF16)
    wide = jnp.dot(stacked, expand, preferred_element_type=F32)
    parts = []
    for hp in range(PAIRS):
        acc = None
        for p in range(len(es)):
            lo_lane = p * ATTN_WIDTH + hp * LANES
            term = wide[:, lo_lane:lo_lane + LANES] * o_refs[p][0, hp]
            acc = term if acc is None else acc + term
        parts.append(acc.astype(BF16))
    mixed = jnp.concatenate(parts + [rec_ref[0]], axis=1)
    x2 = x_ref[0] + jnp.dot(mixed, wo_ref[...], preferred_element_type=F32)
    out_ref[0] = _swiglu_half_step(x2, nw_ref[...], w1_ref, w3_ref, w2_ref)


def _out_ffn2(x3d, os_, rec, wo, nw, w1, w3, w2, tm):
    b, s, d = x3d.shape
    f = w1.shape[1]
    a = ATTN_WIDTH
    row = lambda i, j: (i, j, 0)
    return pl.pallas_call(
        _out_ffn2_kernel,
        grid=(b, s // tm),
        in_specs=[pl.BlockSpec((1, tm, d), row)]
                 + [pl.BlockSpec((1, ATTN_SLABS, tm, LANES), lambda i, j: (i, 0, j, 0))] * 3
                 + [pl.BlockSpec((1, tm, HG_WIDTH), row),
                    _const_spec((a + HG_WIDTH, d)),
                    _const_spec((1, d)),
                    _const_spec((d, f)),
                    _const_spec((d, f)),
                    _const_spec((f, d))],
        out_specs=pl.BlockSpec((1, tm, d), row),
        out_shape=jax.ShapeDtypeStruct((b, s, d), F32),
        compiler_params=pltpu.CompilerParams(
            dimension_semantics=("arbitrary", "arbitrary"), vmem_limit_bytes=VMEM_LIMIT),
        name="out_ffn2",
    )(x3d, *os_, rec, wo, nw, w1, w3, w2)


def _rope_lane_tables(s):
    half = ROPE_DIM // 2
    inv = ROPE_THETA ** (-jnp.arange(0, ROPE_DIM, 2, dtype=F32) / ROPE_DIM)
    ang = jnp.arange(s, dtype=F32)[:, None] * inv[None, :]
    cos, sin = jnp.cos(ang), jnp.sin(ang)
    dim = jnp.arange(LANES) % HEAD_DIM
    idx = dim % half
    c = jnp.where(dim[None, :] < ROPE_DIM, cos[:, idx], 1.0)
    s_lo = jnp.where(dim[None, :] < half, -sin[:, idx], 0.0)
    s_hi = jnp.where((dim[None, :] >= half) & (dim[None, :] < ROPE_DIM), sin[:, idx], 0.0)
    return c.astype(F32), s_lo.astype(F32), s_hi.astype(F32)


def kernel(x, ffn1_norm, ffn1_w1, ffn1_w3, ffn1_w2, mix_norm, w_in, q_norm, k_norm,
           hg_lb_logits, hg_out_norm, w_out, ffn2_norm, ffn2_w1, ffn2_w3, ffn2_w2):
    b, s, d = x.shape
    depth = ffn1_norm.shape[0]
    tm = 512
    cos, s_lo, s_hi = _rope_lane_tables(s)
    head_of = jnp.arange(2 * LANES) // HEAD_DIM
    seg = (head_of[:, None] == head_of[None, :]).astype(BF16)
    for layer in range(depth):
        x1 = _ffn1(x.reshape(b * s, d), ffn1_norm[layer][None, :], ffn1_w1[layer].astype(BF16),
                   ffn1_w3[layer].astype(BF16), ffn1_w2[layer].astype(BF16), tm).reshape(b, s, d)
        (q1, k1, v1, q4, k4, v4, q16, k16, v16, hq, hf, hi, hg) = _in_proj(
            x1, mix_norm[layer][None, :], w_in[layer].astype(BF16), seg,
            jnp.tile(q_norm[layer], ATTN_HEADS)[None, :], jnp.tile(k_norm[layer], ATTN_HEADS)[None, :],
            cos, s_lo, s_hi, tm)
        os_ = [_attention(q, k, v) for q, k, v in
               ((q1[:, None], k1[:, None], v1[:, None]), (q4, k4, v4), (q16, k16, v16))]
        rec = _hgrn2(hg_lb_logits, hq, hf, hi, hg, hg_out_norm[layer][None, :], layer)
        x = _out_ffn2(x1, os_, rec, w_out[layer].astype(BF16),
                      ffn2_norm[layer][None, :], ffn2_w1[layer].astype(BF16),
                      ffn2_w3[layer].astype(BF16), ffn2_w2[layer].astype(BF16), tm)
    return x
```

```python
import functools

import jax
import jax.numpy as jnp
from jax import lax
from jax.experimental import pallas as pl
from jax.experimental.pallas import tpu as pltpu

F32 = jnp.float32
BF16 = jnp.bfloat16

EPS = 1e-6
NEG_INF = -1e30
LOG2_E = 1.4426950408889634
HEAD_DIM = 64
ATTN_HEADS = 8
ATTN_WIDTH = ATTN_HEADS * HEAD_DIM
ROPE_DIM = HEAD_DIM // 4
ROPE_THETA = 500000.0
DILATIONS = (1, 4, 16)
HG_HEADS = 4
HG_DIM = 128
HG_CHUNK = 64
HG_WIDTH = HG_HEADS * HG_DIM

LANES = 128
ATTN_BLOCK = 128
PAIRS = ATTN_WIDTH // LANES
ATTN_CHUNK = 4
ATTN_SLABS = PAIRS + 2
MIX_SUBTILES = 2
VMEM_LIMIT = 56 * 1024 * 1024


def _const_spec(shape):
    nd = len(shape)
    return pl.BlockSpec(shape, lambda *_: (0,) * nd, pipeline_mode=pl.Buffered(1))


def _rms_norm_rows(x, w):
    return x * lax.rsqrt(jnp.mean(x * x, axis=-1, keepdims=True) + EPS) * w


def _silu(a):
    return a * (1.0 / (1.0 + jnp.exp(-a)))


def _swiglu_half_step(x, nw, w1_ref, w3_ref, w2_ref):
    h = _rms_norm_rows(x, nw).astype(BF16)
    a = jnp.dot(h, w1_ref[...], preferred_element_type=F32)
    b = jnp.dot(h, w3_ref[...], preferred_element_type=F32)
    g = (_silu(a) * b).astype(BF16)
    y = jnp.dot(g, w2_ref[...], preferred_element_type=F32)
    return x + 0.5 * y


def _split_bf16(x):
    hi = x.astype(BF16)
    return hi, (x - hi.astype(F32)).astype(BF16)


def _ffn1_kernel(x_ref, nw_ref, w1_ref, w3_ref, w2_ref, o_ref):
    o_ref[...] = _swiglu_half_step(x_ref[...], nw_ref[...], w1_ref, w3_ref, w2_ref)


def _ffn1(x2d, nw, w1, w3, w2, tm):
    n, d = x2d.shape
    f = w1.shape[1]
    return pl.pallas_call(
        _ffn1_kernel,
        grid=(n // tm,),
        in_specs=[
            pl.BlockSpec((tm, d), lambda i: (i, 0)),
            _const_spec((1, d)),
            _const_spec((d, f)),
            _const_spec((d, f)),
            _const_spec((f, d)),
        ],
        out_specs=pl.BlockSpec((tm, d), lambda i: (i, 0)),
        out_shape=jax.ShapeDtypeStruct((n, d), F32),
        compiler_params=pltpu.CompilerParams(
            dimension_semantics=("arbitrary",), vmem_limit_bytes=VMEM_LIMIT),
        name="ffn1",
    )(x2d, nw, w1, w3, w2)


def _head_norm_rope(t, seg_ref, w, cos, sin_lo, sin_hi, scale):
    sq = (t * t).astype(BF16)
    seg = seg_ref[...]
    sw = seg.shape[0]
    ms = jnp.concatenate(
        [jnp.dot(sq[:, c:c + sw], seg, preferred_element_type=F32)
         for c in range(0, ATTN_WIDTH, sw)], axis=1) * (1.0 / HEAD_DIM)
    y = t * lax.rsqrt(ms + EPS) * w
    outs = []
    for c in range(PAIRS):
        yc = y[:, c * LANES:(c + 1) * LANES]
        from_hi = pltpu.roll(yc, LANES - ROPE_DIM // 2, axis=1)
        from_lo = pltpu.roll(yc, ROPE_DIM // 2, axis=1)
        outs.append((yc * cos + from_hi * sin_lo + from_lo * sin_hi) * scale)
    return outs


def _store_attn_input(slabs, ti, rows, sub, ts, outs, slab_scr, res4_scr):
    o1, o4, o16 = outs
    for c in range(PAIRS):
        lanes = slice(c * LANES, (c + 1) * LANES)
        o1[0, rows, lanes] = slabs[c].astype(BF16)
        sl = ti * PAIRS + c
        slab_scr[sub, sl] = slabs[c]
        for r in range(4):
            rows4 = slab_scr[sub, sl, pl.ds(r, ts // 4, stride=4), :]
            o4[0, r, sub * (ts // 4):(sub + 1) * (ts // 4), lanes] = rows4.astype(BF16)
            res4_scr[sub, sl, r] = rows4
            for a_ in range(4):
                rows16 = res4_scr[sub, sl, r, pl.ds(a_, ts // 16, stride=4), :]
                o16[0, 4 * a_ + r, sub * (ts // 16):(sub + 1) * (ts // 16), lanes] = (
                    rows16.astype(BF16))


def _hgrn_head_pieces(p_at, head, lb, nw, state_scr, seq_start, rec_ref, tm):
    c = HG_CHUNK
    t2 = 2 * c
    base = 3 * ATTN_WIDTH + head * HG_DIM
    col = lambda grp: base + grp * HG_WIDTH
    atts, qds, vs, entering = [], [], [], []

    def masks():
        row = lax.broadcasted_iota(jnp.int32, (t2, HG_DIM), 0)
        return row % c, row < c

    def scores():
        rin, first_chunk = masks()
        cc = lax.broadcasted_iota(jnp.int32, (t2, t2), 1)
        rr = lax.broadcasted_iota(jnp.int32, (t2, t2), 0)
        tril = ((rr // c) == (cc // c)) & ((cc % c) <= (rr % c))
        css, decs = [], []
        for t in range(tm // t2):
            rows = slice(t * t2, (t + 1) * t2)
            score_group(rows, rin, first_chunk, tril, css, decs)
        state = jnp.where(seq_start, 0.0, state_scr[head])
        for ch in range(tm // c):
            entering.append(state.astype(BF16))
            state = state * decs[ch] + css[ch]
        state_scr[head] = state

    def score_group(rows, rin, first_chunk, tril, css, decs):
        f = lb + (1.0 - lb) * (1.0 / (1.0 + jnp.exp(-p_at(rows, col(1), HG_DIM))))
        gl = jnp.log(f)
        kk = 1.0 - f
        for sft in (1, 2, 4, 8, 16, 32):
            gl = gl + jnp.where(rin >= sft, pltpu.roll(gl, sft, axis=0), 0.0)
        g_last = jnp.where(first_chunk, gl[c - 1:c, :], gl[t2 - 1:t2, :])
        qd = (p_at(rows, col(0), HG_DIM) * jnp.exp(gl)).astype(BF16)
        kd = (kk * jnp.exp(-gl)).astype(BF16)
        kl = (kk * jnp.exp(g_last - gl)).astype(BF16)
        v = p_at(rows, col(2), HG_DIM).astype(BF16)
        att = lax.dot_general(qd, kd, (((1,), (1,)), ((), ())), preferred_element_type=F32)
        atts.append(jnp.where(tril, att, 0.0).astype(BF16))
        zero = jnp.zeros_like(kl)
        kl2 = jnp.concatenate([jnp.where(first_chunk, kl, zero), jnp.where(first_chunk, zero, kl)],
                              axis=1)
        cs2 = lax.dot_general(v, kl2, (((0,), (0,)), ((), ())), preferred_element_type=F32)
        css += [cs2[:, :HG_DIM], cs2[:, HG_DIM:]]
        decs += [jnp.exp(gl[c - 1:c, :]), jnp.exp(gl[t2 - 1:t2, :])]
        qds.append(qd)
        vs.append(v)

    def outputs():
        _, first_chunk = masks()
        for t in range(tm // t2):
            rows = slice(t * t2, (t + 1) * t2)
            intra = jnp.dot(atts[t], vs[t], preferred_element_type=F32)
            states = jnp.concatenate([entering[2 * t], entering[2 * t + 1]], axis=0)
            inter2 = lax.dot_general(qds[t], states, (((1,), (1,)), ((), ())),
                                     preferred_element_type=F32)
            rec = intra + jnp.where(first_chunk, inter2[:, :HG_DIM], inter2[:, HG_DIM:])
            gate = p_at(rows, col(3), HG_DIM)
            rec_ref[0, rows, head * HG_DIM:(head + 1) * HG_DIM] = (
                _rms_norm_rows(rec, nw) * _silu(gate)).astype(BF16)

    return scores, outputs


def _mix_in_kernel(x_ref, nw_ref, w_ref, seg_ref, qw_ref, kw_ref, cos_ref, slo_ref, shi_ref,
                   lbl_ref, hnw_ref,
                   q1_ref, k1_ref, v1_ref, q4_ref, k4_ref, v4_ref, q16_ref, k16_ref, v16_ref,
                   rec_ref, pa_scr, pb_scr, slab_scr, res4_scr, state_scr,
                   *, layer, tiles_per_seq):
    s = pl.program_id(0)
    tm = x_ref.shape[0]
    cols = w_ref.shape[1]
    nsub = slab_scr.shape[0]
    ts = tm // nsub

    @pl.when(s == 0)
    def _():
        pb_scr[...] = jnp.zeros(pb_scr.shape, F32)
        state_scr[...] = jnp.zeros(state_scr.shape, F32)

    def step(write_scr, read_scr):
        a = ATTN_WIDTH
        normed = {}

        def project(sub, c0):
            rows = slice(sub * ts, (sub + 1) * ts)
            if sub not in normed:
                normed[sub] = _rms_norm_rows(x_ref[rows, :], nw_ref[...]).astype(BF16)
            write_scr[rows, c0:c0 + a] = jnp.dot(normed[sub], w_ref[:, c0:c0 + a],
                                                 preferred_element_type=F32)

        p_at = lambda rows, c0, width: read_scr[rows, c0:c0 + width]
        outs = ((q1_ref, q4_ref, q16_ref), (k1_ref, k4_ref, k16_ref), (v1_ref, v4_ref, v16_ref))

        def attn_input(sub, ti):
            rows = slice(sub * ts, (sub + 1) * ts)
            if ti == 2:
                slabs = [p_at(rows, 2 * a + c * LANES, LANES) for c in range(PAIRS)]
            else:
                gain, scale = ((qw_ref, LOG2_E * HEAD_DIM ** -0.5), (kw_ref, 1.0))[ti]
                slabs = _head_norm_rope(p_at(rows, ti * a, a), seg_ref, gain[...], cos_ref[rows, :],
                                        slo_ref[rows, :], shi_ref[rows, :], scale)
            _store_attn_input(slabs, ti, rows, sub, ts, outs[ti], slab_scr, res4_scr)

        lg = lbl_ref[...]
        e = jnp.exp(lg - jnp.max(lg, axis=0, keepdims=True))
        lb = jnp.sum(e[0:layer + 1, :], axis=0, keepdims=True) / jnp.sum(e, axis=0, keepdims=True)
        seq_start = (jnp.maximum(s - 1, 0) % tiles_per_seq) == 0

        finishing = [functools.partial(attn_input, sub, ti) for sub in range(nsub) for ti in range(3)]
        for head in range(HG_HEADS):
            finishing += _hgrn_head_pieces(p_at, head, lb[:, head * HG_DIM:(head + 1) * HG_DIM],
                                           hnw_ref[...], state_scr, seq_start, rec_ref, tm)
        projecting = [functools.partial(project, sub, c0)
                      for sub in range(nsub) for c0 in range(0, cols, a)]
        for i in range(max(len(projecting), len(finishing))):
            if i < len(projecting):
                projecting[i]()
            if i < len(finishing):
                finishing[i]()

    pl.when(s % 2 == 0)(lambda: step(pa_scr, pb_scr))
    pl.when(s % 2 == 1)(lambda: step(pb_scr, pa_scr))


def _mix_in(x2d, nw, w_in, seg, qw, kw, cos, slo, shi, lb_logits, hnw, batch, tm, layer):
    n, d = x2d.shape
    s = n // batch
    cols = w_in.shape[1]
    a = ATTN_WIDTH
    ntile = n // tm
    per_seq = s // tm
    ts = tm // MIX_SUBTILES
    done = lambda i: jnp.maximum(i - 1, 0)
    row = lambda i: (done(i) // per_seq, done(i) % per_seq, 0)
    out_specs, out_shape = [], []
    for dil in DILATIONS:
        for _ in range(3):
            if dil == 1:
                out_specs.append(pl.BlockSpec((1, tm, a), row))
                out_shape.append(jax.ShapeDtypeStruct((batch, s, a), BF16))
            else:
                out_specs.append(pl.BlockSpec(
                    (1, dil, tm // dil, a), lambda i: (done(i) // per_seq, 0, done(i) % per_seq, 0)))
                out_shape.append(jax.ShapeDtypeStruct((batch, dil, s // dil, a), BF16))
    out_specs.append(pl.BlockSpec((1, tm, HG_WIDTH), row))
    out_shape.append(jax.ShapeDtypeStruct((batch, s, HG_WIDTH), BF16))
    pos = lambda i: (done(i) % per_seq, 0)
    return pl.pallas_call(
        functools.partial(_mix_in_kernel, layer=layer, tiles_per_seq=per_seq),
        grid=(ntile + 1,),
        in_specs=[
            pl.BlockSpec((tm, d), lambda i: (jnp.minimum(i, ntile - 1), 0)),
            _const_spec((1, d)),
            _const_spec((d, cols)),
            _const_spec(seg.shape),
            _const_spec((1, a)),
            _const_spec((1, a)),
            pl.BlockSpec((tm, LANES), pos),
            pl.BlockSpec((tm, LANES), pos),
            pl.BlockSpec((tm, LANES), pos),
            _const_spec(lb_logits.shape),
            _const_spec((1, HG_DIM)),
        ],
        out_specs=out_specs,
        out_shape=out_shape,
        scratch_shapes=[pltpu.VMEM((tm, cols), F32),
                        pltpu.VMEM((tm, cols), F32),
                        pltpu.VMEM((MIX_SUBTILES, 3 * PAIRS, ts, LANES), F32),
                        pltpu.VMEM((MIX_SUBTILES, 3 * PAIRS, 4, ts // 4, LANES), F32),
                        pltpu.VMEM((HG_HEADS, HG_DIM, HG_DIM), F32)],
        compiler_params=pltpu.CompilerParams(
            dimension_semantics=("arbitrary",), vmem_limit_bytes=VMEM_LIMIT),
        name="mix_in",
    )(x2d, nw, w_in, seg, qw, kw, cos, slo, shi, lb_logits, hnw)


def _attn_kernel(q_ref, k_ref, v_ref, o_ref, p_scr, *, dilation, nblk):
    w = ATTN_BLOCK
    groups = dilation
    lane = lax.broadcasted_iota(jnp.int32, (w, LANES), 1)
    head0 = lane < HEAD_DIM
    qi1 = lax.broadcasted_iota(jnp.int32, (2 * w, w), 0) % w
    kj1 = lax.broadcasted_iota(jnp.int32, (2 * w, w), 1)
    valid_first = kj1 <= qi1
    qi2 = lax.broadcasted_iota(jnp.int32, (2 * w, 2 * w), 0) % w
    kj2 = lax.broadcasted_iota(jnp.int32, (2 * w, 2 * w), 1)
    dist = qi2 + w - kj2
    valid_band = (dist >= 0) & (dist <= w)

    def store_rows(slab, r0, g, val):
        if dilation == 1:
            o_ref[0, slab, pl.ds(r0, w), :] = val
        else:
            o_ref[0, slab, pl.ds(r0 * dilation + g, w, stride=dilation), :] = val

    def score_block(g, qb, r0, rows_k, nk, valid):
        rows_q = pl.ds(r0, w)
        m_acc = jnp.zeros((w, LANES), F32)
        for hp in range(PAIRS):
            lanes = slice(hp * LANES, (hp + 1) * LANES)
            q2 = q_ref[0, g, rows_q, lanes]
            zero = jnp.zeros_like(q2)
            qq = jnp.concatenate([jnp.where(head0, q2, zero), jnp.where(head0, zero, q2)], axis=0)
            s = lax.dot_general(qq, k_ref[0, g, rows_k, lanes], (((1,), (1,)), ((), ())),
                                preferred_element_type=F32)
            s = jnp.where(valid, s, NEG_INF)
            m = jnp.max(s, axis=1, keepdims=True)
            p_scr[g * nblk + qb, hp, :, 0:nk] = jnp.exp2(s - m).astype(BF16)
            for h in range(2):
                m_acc = jnp.where(lane == 2 * hp + h, m[h * w:(h + 1) * w], m_acc)
        store_rows(PAIRS, r0, g, m_acc)

    def value_block(g, qb, r0, rows_k, nk):
        d_acc = jnp.zeros((w, LANES), F32)
        ones = jnp.ones((nk, LANES), BF16)
        for hp in range(PAIRS):
            lanes = slice(hp * LANES, (hp + 1) * LANES)
            vx = jnp.concatenate([v_ref[0, g, rows_k, lanes], ones], axis=1)
            r = jnp.dot(p_scr[g * nblk + qb, hp, :, 0:nk], vx, preferred_element_type=F32)
            store_rows(hp, r0, g, jnp.where(head0, r[:w, :LANES], r[w:, :LANES]))
            for h in range(2):
                d_acc = jnp.where(lane == 2 * hp + h, r[h * w:(h + 1) * w, LANES:], d_acc)
        store_rows(PAIRS + 1, r0, g, d_acc)

    def run_chunk(blocks):
        args = []
        for g, qb in blocks:
            if isinstance(qb, int) and qb == 0:
                args.append((g, 0, 0, pl.ds(0, w), w, valid_first))
            else:
                r0 = qb * w if isinstance(qb, int) else pl.multiple_of(qb * w, w)
                args.append((g, qb, r0, pl.ds(r0 - w, 2 * w), 2 * w, valid_band))
        for g, qb, r0, rows_k, nk, valid in args:
            score_block(g, qb, r0, rows_k, nk, valid)
        for g, qb, r0, rows_k, nk, _ in args:
            value_block(g, qb, r0, rows_k, nk)

    def loop_chunks(lo, hi, blocks_of):
        def body(i, carry):
            run_chunk(blocks_of(i))
            return carry
        lax.fori_loop(lo, hi, body, 0)

    c = ATTN_CHUNK
    if nblk == 1:
        loop_chunks(0, groups // c, lambda i: [(i * c + j, 0) for j in range(c)])
    elif nblk == c:
        loop_chunks(0, groups, lambda g: [(g, qb) for qb in range(c)])
    else:
        run_chunk([(0, qb) for qb in range(c)])
        loop_chunks(1, nblk // c, lambda i: [(0, i * c + j) for j in range(c)])


def _attention(q, k, v):
    b, dilation, l, a = q.shape
    s = l * dilation
    nblk = l // ATTN_BLOCK
    spec = pl.BlockSpec((1, dilation, l, a), lambda i: (i, 0, 0, 0))
    return pl.pallas_call(
        functools.partial(_attn_kernel, dilation=dilation, nblk=nblk),
        grid=(b,),
        in_specs=[spec, spec, spec],
        out_specs=pl.BlockSpec((1, ATTN_SLABS, s, LANES), lambda i: (i, 0, 0, 0)),
        out_shape=jax.ShapeDtypeStruct((b, ATTN_SLABS, s, LANES), F32),
        scratch_shapes=[pltpu.VMEM((dilation * nblk, PAIRS, 2 * ATTN_BLOCK, 2 * ATTN_BLOCK), BF16)],
        compiler_params=pltpu.CompilerParams(
            dimension_semantics=("arbitrary",), vmem_limit_bytes=VMEM_LIMIT),
        name=f"attn_d{dilation}",
    )(q, k, v)


def _out_ffn2_kernel(x_ref, o1_ref, o2_ref, o3_ref, rec_ref,
                     wo_ref, nw_ref, w1_ref, w3_ref, w2_ref, out_ref):
    o_refs = (o1_ref, o2_ref, o3_ref)
    ms = [r[0, PAIRS, :, 0:ATTN_HEADS] for r in o_refs]
    dens = [r[0, PAIRS + 1, :, 0:ATTN_HEADS] for r in o_refs]
    mx = jnp.maximum(jnp.maximum(ms[0], ms[1]), ms[2])
    es = [jnp.exp2(m - mx) for m in ms]
    inv = 1.0 / (es[0] * dens[0] + es[1] * dens[1] + es[2] * dens[2])
    halves = [h for e in es for h in _split_bf16(e * inv)]
    stacked = jnp.concatenate(halves, axis=1)
    kdim = stacked.shape[1]
    src = lax.broadcasted_iota(jnp.int32, (kdim, len(es) * ATTN_WIDTH), 0)
    dst = lax.broadcasted_iota(jnp.int32, (kdim, len(es) * ATTN_WIDTH), 1)
    expand = ((src // (2 * ATTN_HEADS) == dst // ATTN_WIDTH)
              & (src % ATTN_HEADS == (dst % ATTN_WIDTH) // HEAD_DIM)).astype(BF16)
    wide = jnp.dot(stacked, expand, preferred_element_type=F32)
    parts = []
    for hp in range(PAIRS):
        acc = None
        for p in range(len(es)):
            lo_lane = p * ATTN_WIDTH + hp * LANES
            term = wide[:, lo_lane:lo_lane + LANES] * o_refs[p][0, hp]
            acc = term if acc is None else acc + term
        parts.append(acc.astype(BF16))
    mixed = jnp.concatenate(parts + [rec_ref[0]], axis=1)
    x2 = x_ref[0] + jnp.dot(mixed, wo_ref[...], preferred_element_type=F32)
    out_ref[0] = _swiglu_half_step(x2, nw_ref[...], w1_ref, w3_ref, w2_ref)


def _out_ffn2(x3d, os_, rec, wo, nw, w1, w3, w2, tm):
    b, s, d = x3d.shape
    f = w1.shape[1]
    a = ATTN_WIDTH
    row = lambda i, j: (i, j, 0)
    return pl.pallas_call(
        _out_ffn2_kernel,
        grid=(b, s // tm),
        in_specs=[pl.BlockSpec((1, tm, d), row)]
                 + [pl.BlockSpec((1, ATTN_SLABS, tm, LANES), lambda i, j: (i, 0, j, 0))] * 3
                 + [pl.BlockSpec((1, tm, HG_WIDTH), row),
                    _const_spec((a + HG_WIDTH, d)),
                    _const_spec((1, d)),
                    _const_spec((d, f)),
                    _const_spec((d, f)),
                    _const_spec((f, d))],
        out_specs=pl.BlockSpec((1, tm, d), row),
        out_shape=jax.ShapeDtypeStruct((b, s, d), F32),
        compiler_params=pltpu.CompilerParams(
            dimension_semantics=("arbitrary", "arbitrary"), vmem_limit_bytes=VMEM_LIMIT),
        name="out_ffn2",
    )(x3d, *os_, rec, wo, nw, w1, w3, w2)


def _rope_lane_tables(s):
    half = ROPE_DIM // 2
    inv = ROPE_THETA ** (-jnp.arange(0, ROPE_DIM, 2, dtype=F32) / ROPE_DIM)
    ang = jnp.arange(s, dtype=F32)[:, None] * inv[None, :]
    cos, sin = jnp.cos(ang), jnp.sin(ang)
    dim = jnp.arange(LANES) % HEAD_DIM
    idx = dim % half
    c = jnp.where(dim[None, :] < ROPE_DIM, cos[:, idx], 1.0)
    s_lo = jnp.where(dim[None, :] < half, -sin[:, idx], 0.0)
    s_hi = jnp.where((dim[None, :] >= half) & (dim[None, :] < ROPE_DIM), sin[:, idx], 0.0)
    return c.astype(F32), s_lo.astype(F32), s_hi.astype(F32)


def kernel(x, ffn1_norm, ffn1_w1, ffn1_w3, ffn1_w2, mix_norm, w_in, q_norm, k_norm,
           hg_lb_logits, hg_out_norm, w_out, ffn2_norm, ffn2_w1, ffn2_w3, ffn2_w2):
    b, s, d = x.shape
    depth = ffn1_norm.shape[0]
    tm = 512
    cos, s_lo, s_hi = _rope_lane_tables(s)
    head_of = jnp.arange(2 * LANES) // HEAD_DIM
    seg = (head_of[:, None] == head_of[None, :]).astype(BF16)
    for layer in range(depth):
        x1 = _ffn1(x.reshape(b * s, d), ffn1_norm[layer][None, :], ffn1_w1[layer].astype(BF16),
                   ffn1_w3[layer].astype(BF16), ffn1_w2[layer].astype(BF16), tm)
        (q1, k1, v1, q4, k4, v4, q16, k16, v16, rec) = _mix_in(
            x1, mix_norm[layer][None, :], w_in[layer].astype(BF16), seg,
            jnp.tile(q_norm[layer], ATTN_HEADS)[None, :], jnp.tile(k_norm[layer], ATTN_HEADS)[None, :],
            cos, s_lo, s_hi, hg_lb_logits, hg_out_norm[layer][None, :], b, tm, layer)
        os_ = [_attention(q, k, v) for q, k, v in
               ((q1[:, None], k1[:, None], v1[:, None]), (q4, k4, v4), (q16, k16, v16))]
        x = _out_ffn2(x1.reshape(b, s, d), os_, rec, w_out[layer].astype(BF16),
                      ffn2_norm[layer][None, :], ffn2_w1[layer].astype(BF16),
                      ffn2_w3[layer].astype(BF16), ffn2_w2[layer].astype(BF16), tm)
    return x
```

```python
import functools

import jax
import jax.numpy as jnp
from jax import lax
from jax.experimental import pallas as pl
from jax.experimental.pallas import tpu as pltpu

F32 = jnp.float32
BF16 = jnp.bfloat16

EPS = 1e-6
NEG_INF = -1e30
LOG2_E = 1.4426950408889634
HEAD_DIM = 64
ATTN_HEADS = 8
ATTN_WIDTH = ATTN_HEADS * HEAD_DIM
ROPE_DIM = HEAD_DIM // 4
ROPE_THETA = 500000.0
DILATIONS = (1, 4, 16)
HG_HEADS = 4
HG_DIM = 128
HG_CHUNK = 64
HG_WIDTH = HG_HEADS * HG_DIM

LANES = 128
ATTN_BLOCK = 128
PAIRS = ATTN_WIDTH // LANES
ATTN_CHUNK = 4
ATTN_SLABS = PAIRS + 2
MIX_SUBTILES = 2
FFN_SUBTILES = 2
VMEM_LIMIT = 56 * 1024 * 1024


def _const_spec(shape):
    nd = len(shape)
    return pl.BlockSpec(shape, lambda *_: (0,) * nd, pipeline_mode=pl.Buffered(1))


def _rms_norm_rows(x, w):
    return x * lax.rsqrt(jnp.mean(x * x, axis=-1, keepdims=True) + EPS) * w


def _silu(a):
    return a * (1.0 / (1.0 + jnp.exp(-a)))


def _swiglu_half_step(xs, nw, w1_ref, w3_ref, w2_ref):
    hs = [_rms_norm_rows(x, nw).astype(BF16) for x in xs]
    ab = [(jnp.dot(h, w1_ref[...], preferred_element_type=F32),
           jnp.dot(h, w3_ref[...], preferred_element_type=F32)) for h in hs]
    outs = []
    for x, (a, b) in zip(xs, ab):
        g = (_silu(a) * b).astype(BF16)
        outs.append(x + 0.5 * jnp.dot(g, w2_ref[...], preferred_element_type=F32))
    return outs


def _split_bf16(x):
    hi = x.astype(BF16)
    return hi, (x - hi.astype(F32)).astype(BF16)


def _row_subtiles(n):
    ts = n // FFN_SUBTILES
    return [slice(i * ts, (i + 1) * ts) for i in range(FFN_SUBTILES)]


def _ffn1_kernel(x_ref, nw_ref, w1_ref, w3_ref, w2_ref, o_ref):
    subs = _row_subtiles(x_ref.shape[0])
    outs = _swiglu_half_step([x_ref[r, :] for r in subs], nw_ref[...], w1_ref, w3_ref, w2_ref)
    for r, o in zip(subs, outs):
        o_ref[r, :] = o


def _ffn1(x2d, nw, w1, w3, w2, tm):
    n, d = x2d.shape
    f = w1.shape[1]
    return pl.pallas_call(
        _ffn1_kernel,
        grid=(n // tm,),
        in_specs=[
            pl.BlockSpec((tm, d), lambda i: (i, 0)),
            _const_spec((1, d)),
            _const_spec((d, f)),
            _const_spec((d, f)),
            _const_spec((f, d)),
        ],
        out_specs=pl.BlockSpec((tm, d), lambda i: (i, 0)),
        out_shape=jax.ShapeDtypeStruct((n, d), F32),
        compiler_params=pltpu.CompilerParams(
            dimension_semantics=("arbitrary",), vmem_limit_bytes=VMEM_LIMIT),
        name="ffn1",
    )(x2d, nw, w1, w3, w2)


def _head_norm_rope(t, seg_ref, w, cos, sin_lo, sin_hi, scale):
    sq = (t * t).astype(BF16)
    seg = seg_ref[...]
    sw = seg.shape[0]
    ms = jnp.concatenate(
        [jnp.dot(sq[:, c:c + sw], seg, preferred_element_type=F32)
         for c in range(0, ATTN_WIDTH, sw)], axis=1) * (1.0 / HEAD_DIM)
    y = t * lax.rsqrt(ms + EPS) * w
    outs = []
    for c in range(PAIRS):
        yc = y[:, c * LANES:(c + 1) * LANES]
        from_hi = pltpu.roll(yc, LANES - ROPE_DIM // 2, axis=1)
        from_lo = pltpu.roll(yc, ROPE_DIM // 2, axis=1)
        outs.append((yc * cos + from_hi * sin_lo + from_lo * sin_hi) * scale)
    return outs


def _store_attn_input(slabs, ti, rows, sub, ts, outs, slab_scr, res4_scr):
    o1, o4, o16 = outs
    lanes = [slice(c * LANES, (c + 1) * LANES) for c in range(PAIRS)]
    for c in range(PAIRS):
        o1[0, rows, lanes[c]] = slabs[c].astype(BF16)
        slab_scr[sub, ti * PAIRS + c] = slabs[c]
    for c in range(PAIRS):
        sl = ti * PAIRS + c
        for r in range(4):
            rows4 = slab_scr[sub, sl, pl.ds(r, ts // 4, stride=4), :]
            o4[0, r, sub * (ts // 4):(sub + 1) * (ts // 4), lanes[c]] = rows4.astype(BF16)
            res4_scr[sub, sl, r] = rows4
    for c in range(PAIRS):
        sl = ti * PAIRS + c
        for r in range(4):
            for a_ in range(4):
                rows16 = res4_scr[sub, sl, r, pl.ds(a_, ts // 16, stride=4), :]
                o16[0, 4 * a_ + r, sub * (ts // 16):(sub + 1) * (ts // 16), lanes[c]] = (
                    rows16.astype(BF16))


def _hgrn_head_pieces(p_at, head, lb, nw, state_scr, seq_start, rec_ref, tm):
    c = HG_CHUNK
    t2 = 2 * c
    ngroup = tm // t2
    base = 3 * ATTN_WIDTH + head * HG_DIM
    col = lambda grp: base + grp * HG_WIDTH
    atts, qds, vs, entering, css, decs = [], [], [], [], [], []

    def masks():
        row = lax.broadcasted_iota(jnp.int32, (t2, HG_DIM), 0)
        return row % c, row < c

    def scan():
        state = jnp.where(seq_start, 0.0, state_scr[head])
        for ch in range(tm // c):
            entering.append(state.astype(BF16))
            state = state * decs[ch] + css[ch]
        state_scr[head] = state

    def score_group(t):
        rows = slice(t * t2, (t + 1) * t2)
        rin, first_chunk = masks()
        cc = lax.broadcasted_iota(jnp.int32, (t2, t2), 1)
        rr = lax.broadcasted_iota(jnp.int32, (t2, t2), 0)
        tril = ((rr // c) == (cc // c)) & ((cc % c) <= (rr % c))
        f = lb + (1.0 - lb) * (1.0 / (1.0 + jnp.exp(-p_at(rows, col(1), HG_DIM))))
        gl = jnp.log(f)
        kk = 1.0 - f
        for sft in (1, 2, 4, 8, 16, 32):
            gl = gl + jnp.where(rin >= sft, pltpu.roll(gl, sft, axis=0), 0.0)
        g_last = jnp.where(first_chunk, gl[c - 1:c, :], gl[t2 - 1:t2, :])
        qd = (p_at(rows, col(0), HG_DIM) * jnp.exp(gl)).astype(BF16)
        kd = (kk * jnp.exp(-gl)).astype(BF16)
        kl = (kk * jnp.exp(g_last - gl)).astype(BF16)
        v = p_at(rows, col(2), HG_DIM).astype(BF16)
        att = lax.dot_general(qd, kd, (((1,), (1,)), ((), ())), preferred_element_type=F32)
        atts.append(jnp.where(tril, att, 0.0).astype(BF16))
        zero = jnp.zeros_like(kl)
        kl2 = jnp.concatenate([jnp.where(first_chunk, kl, zero), jnp.where(first_chunk, zero, kl)],
                              axis=1)
        cs2 = lax.dot_general(v, kl2, (((0,), (0,)), ((), ())), preferred_element_type=F32)
        css.extend([cs2[:, :HG_DIM], cs2[:, HG_DIM:]])
        decs.extend([jnp.exp(gl[c - 1:c, :]), jnp.exp(gl[t2 - 1:t2, :])])
        qds.append(qd)
        vs.append(v)

    def output_group(t):
        rows = slice(t * t2, (t + 1) * t2)
        _, first_chunk = masks()
        intra = jnp.dot(atts[t], vs[t], preferred_element_type=F32)
        states = jnp.concatenate([entering[2 * t], entering[2 * t + 1]], axis=0)
        inter2 = lax.dot_general(qds[t], states, (((1,), (1,)), ((), ())),
                                 preferred_element_type=F32)
        rec = intra + jnp.where(first_chunk, inter2[:, :HG_DIM], inter2[:, HG_DIM:])
        gate = p_at(rows, col(3), HG_DIM)
        rec_ref[0, rows, head * HG_DIM:(head + 1) * HG_DIM] = (
            _rms_norm_rows(rec, nw) * _silu(gate)).astype(BF16)

    def scores():
        for t in range(ngroup):
            score_group(t)
        scan()

    def outputs():
        for t in range(ngroup):
            output_group(t)

    return [scores, outputs]


def _mix_in_kernel(x_ref, nw_ref, w_ref, seg_ref, qw_ref, kw_ref, cos_ref, slo_ref, shi_ref,
                   lbl_ref, hnw_ref,
                   q1_ref, k1_ref, v1_ref, q4_ref, k4_ref, v4_ref, q16_ref, k16_ref, v16_ref,
                   rec_ref, pa_scr, pb_scr, slab_scr, res4_scr, state_scr,
                   *, layer, tiles_per_seq):
    s = pl.program_id(0)
    tm = x_ref.shape[0]
    cols = w_ref.shape[1]
    nsub = slab_scr.shape[0]
    ts = tm // nsub

    @pl.when(s == 0)
    def _():
        pb_scr[...] = jnp.zeros(pb_scr.shape, F32)
        state_scr[...] = jnp.zeros(state_scr.shape, F32)

    def step(write_scr, read_scr):
        a = ATTN_WIDTH
        normed = {}

        pw = ATTN_WIDTH

        def project(sub, c0):
            rows = slice(sub * ts, (sub + 1) * ts)
            if sub not in normed:
                normed[sub] = _rms_norm_rows(x_ref[rows, :], nw_ref[...]).astype(BF16)
            write_scr[rows, c0:c0 + pw] = jnp.dot(normed[sub], w_ref[:, c0:c0 + pw],
                                                  preferred_element_type=F32)

        p_at = lambda rows, c0, width: read_scr[rows, c0:c0 + width]
        outs = ((q1_ref, q4_ref, q16_ref), (k1_ref, k4_ref, k16_ref), (v1_ref, v4_ref, v16_ref))

        def attn_input(sub, ti):
            rows = slice(sub * ts, (sub + 1) * ts)
            if ti == 2:
                slabs = [p_at(rows, 2 * a + c * LANES, LANES) for c in range(PAIRS)]
            else:
                gain, scale = ((qw_ref, LOG2_E * HEAD_DIM ** -0.5), (kw_ref, 1.0))[ti]
                slabs = _head_norm_rope(p_at(rows, ti * a, a), seg_ref, gain[...], cos_ref[rows, :],
                                        slo_ref[rows, :], shi_ref[rows, :], scale)
            _store_attn_input(slabs, ti, rows, sub, ts, outs[ti], slab_scr, res4_scr)

        lg = lbl_ref[...]
        e = jnp.exp(lg - jnp.max(lg, axis=0, keepdims=True))
        lb = jnp.sum(e[0:layer + 1, :], axis=0, keepdims=True) / jnp.sum(e, axis=0, keepdims=True)
        seq_start = (jnp.maximum(s - 1, 0) % tiles_per_seq) == 0

        attn_pieces = [functools.partial(attn_input, sub, ti)
                       for sub in range(nsub) for ti in range(3)]
        hgrn_pieces = []
        for head in range(HG_HEADS):
            hgrn_pieces += _hgrn_head_pieces(p_at, head, lb[:, head * HG_DIM:(head + 1) * HG_DIM],
                                             hnw_ref[...], state_scr, seq_start, rec_ref, tm)
        finishing = []
        for i in range(max(len(attn_pieces), len(hgrn_pieces))):
            finishing += hgrn_pieces[i:i + 1] + attn_pieces[i:i + 1]
        projecting = [functools.partial(project, sub, c0)
                      for sub in range(nsub) for c0 in range(0, cols, pw)]
        done_f = 0
        for i, piece in enumerate(projecting):
            piece()
            upto = (i + 1) * len(finishing) // len(projecting)
            for f_piece in finishing[done_f:upto]:
                f_piece()
            done_f = upto

    pl.when(s % 2 == 0)(lambda: step(pa_scr, pb_scr))
    pl.when(s % 2 == 1)(lambda: step(pb_scr, pa_scr))


def _mix_in(x2d, nw, w_in, seg, qw, kw, cos, slo, shi, lb_logits, hnw, batch, tm, layer):
    n, d = x2d.shape
    s = n // batch
    cols = w_in.shape[1]
    a = ATTN_WIDTH
    ntile = n // tm
    per_seq = s // tm
    ts = tm // MIX_SUBTILES
    done = lambda i: jnp.maximum(i - 1, 0)
    row = lambda i: (done(i) // per_seq, done(i) % per_seq, 0)
    out_specs, out_shape = [], []
    for dil in DILATIONS:
        for _ in range(3):
            if dil == 1:
                out_specs.append(pl.BlockSpec((1, tm, a), row))
                out_shape.append(jax.ShapeDtypeStruct((batch, s, a), BF16))
            else:
                out_specs.append(pl.BlockSpec(
                    (1, dil, tm // dil, a), lambda i: (done(i) // per_seq, 0, done(i) % per_seq, 0)))
                out_shape.append(jax.ShapeDtypeStruct((batch, dil, s // dil, a), BF16))
    out_specs.append(pl.BlockSpec((1, tm, HG_WIDTH), row))
    out_shape.append(jax.ShapeDtypeStruct((batch, s, HG_WIDTH), BF16))
    pos = lambda i: (done(i) % per_seq, 0)
    return pl.pallas_call(
        functools.partial(_mix_in_kernel, layer=layer, tiles_per_seq=per_seq),
        grid=(ntile + 1,),
        in_specs=[
            pl.BlockSpec((tm, d), lambda i: (jnp.minimum(i, ntile - 1), 0)),
            _const_spec((1, d)),
            _const_spec((d, cols)),
            _const_spec(seg.shape),
            _const_spec((1, a)),
            _const_spec((1, a)),
            pl.BlockSpec((tm, LANES), pos),
            pl.BlockSpec((tm, LANES), pos),
            pl.BlockSpec((tm, LANES), pos),
            _const_spec(lb_logits.shape),
            _const_spec((1, HG_DIM)),
        ],
        out_specs=out_specs,
        out_shape=out_shape,
        scratch_shapes=[pltpu.VMEM((tm, cols), F32),
                        pltpu.VMEM((tm, cols), F32),
                        pltpu.VMEM((MIX_SUBTILES, 3 * PAIRS, ts, LANES), F32),
                        pltpu.VMEM((MIX_SUBTILES, 3 * PAIRS, 4, ts // 4, LANES), F32),
                        pltpu.VMEM((HG_HEADS, HG_DIM, HG_DIM), F32)],
        compiler_params=pltpu.CompilerParams(
            dimension_semantics=("arbitrary",), vmem_limit_bytes=VMEM_LIMIT),
        name="mix_in",
    )(x2d, nw, w_in, seg, qw, kw, cos, slo, shi, lb_logits, hnw)


def _attn_kernel(q_ref, k_ref, v_ref, o_ref, p_scr, *, dilation, nblk):
    w = ATTN_BLOCK
    groups = dilation
    lane = lax.broadcasted_iota(jnp.int32, (w, LANES), 1)
    head0 = lane < HEAD_DIM
    qi1 = lax.broadcasted_iota(jnp.int32, (2 * w, w), 0) % w
    kj1 = lax.broadcasted_iota(jnp.int32, (2 * w, w), 1)
    valid_first = kj1 <= qi1
    qi2 = lax.broadcasted_iota(jnp.int32, (2 * w, 2 * w), 0) % w
    kj2 = lax.broadcasted_iota(jnp.int32, (2 * w, 2 * w), 1)
    dist = qi2 + w - kj2
    valid_band = (dist >= 0) & (dist <= w)

    def store_rows(slab, r0, g, val):
        if dilation == 1:
            o_ref[0, slab, pl.ds(r0, w), :] = val
        else:
            o_ref[0, slab, pl.ds(r0 * dilation + g, w, stride=dilation), :] = val

    def score_block(g, qb, r0, rows_k, nk, valid):
        rows_q = pl.ds(r0, w)
        m_acc = jnp.zeros((w, LANES), F32)
        for hp in range(PAIRS):
            lanes = slice(hp * LANES, (hp + 1) * LANES)
            q2 = q_ref[0, g, rows_q, lanes]
            zero = jnp.zeros_like(q2)
            qq = jnp.concatenate([jnp.where(head0, q2, zero), jnp.where(head0, zero, q2)], axis=0)
            s = lax.dot_general(qq, k_ref[0, g, rows_k, lanes], (((1,), (1,)), ((), ())),
                                preferred_element_type=F32)
            s = jnp.where(valid, s, NEG_INF)
            m = jnp.max(s, axis=1, keepdims=True)
            p_scr[g * nblk + qb, hp, :, 0:nk] = jnp.exp2(s - m).astype(BF16)
            for h in range(2):
                m_acc = jnp.where(lane == 2 * hp + h, m[h * w:(h + 1) * w], m_acc)
        store_rows(PAIRS, r0, g, m_acc)

    def value_block(g, qb, r0, rows_k, nk):
        d_acc = jnp.zeros((w, LANES), F32)
        ones = jnp.ones((nk, LANES), BF16)
        for hp in range(PAIRS):
            lanes = slice(hp * LANES, (hp + 1) * LANES)
            vx = jnp.concatenate([v_ref[0, g, rows_k, lanes], ones], axis=1)
            r = jnp.dot(p_scr[g * nblk + qb, hp, :, 0:nk], vx, preferred_element_type=F32)
            store_rows(hp, r0, g, jnp.where(head0, r[:w, :LANES], r[w:, :LANES]))
            for h in range(2):
                d_acc = jnp.where(lane == 2 * hp + h, r[h * w:(h + 1) * w, LANES:], d_acc)
        store_rows(PAIRS + 1, r0, g, d_acc)

    def run_chunk(blocks):
        args = []
        for g, qb in blocks:
            if isinstance(qb, int) and qb == 0:
                args.append((g, 0, 0, pl.ds(0, w), w, valid_first))
            else:
                r0 = qb * w if isinstance(qb, int) else pl.multiple_of(qb * w, w)
                args.append((g, qb, r0, pl.ds(r0 - w, 2 * w), 2 * w, valid_band))
        for g, qb, r0, rows_k, nk, valid in args:
            score_block(g, qb, r0, rows_k, nk, valid)
        for g, qb, r0, rows_k, nk, _ in args:
            value_block(g, qb, r0, rows_k, nk)

    def loop_chunks(lo, hi, blocks_of):
        def body(i, carry):
            run_chunk(blocks_of(i))
            return carry
        lax.fori_loop(lo, hi, body, 0)

    c = ATTN_CHUNK
    if nblk == 1:
        loop_chunks(0, groups // c, lambda i: [(i * c + j, 0) for j in range(c)])
    elif nblk == c:
        loop_chunks(0, groups, lambda g: [(g, qb) for qb in range(c)])
    else:
        run_chunk([(0, qb) for qb in range(c)])
        loop_chunks(1, nblk // c, lambda i: [(0, i * c + j) for j in range(c)])


def _attention(q, k, v):
    b, dilation, l, a = q.shape
    s = l * dilation
    nblk = l // ATTN_BLOCK
    spec = pl.BlockSpec((1, dilation, l, a), lambda i: (i, 0, 0, 0))
    return pl.pallas_call(
        functools.partial(_attn_kernel, dilation=dilation, nblk=nblk),
        grid=(b,),
        in_specs=[spec, spec, spec],
        out_specs=pl.BlockSpec((1, ATTN_SLABS, s, LANES), lambda i: (i, 0, 0, 0)),
        out_shape=jax.ShapeDtypeStruct((b, ATTN_SLABS, s, LANES), F32),
        scratch_shapes=[pltpu.VMEM((dilation * nblk, PAIRS, 2 * ATTN_BLOCK, 2 * ATTN_BLOCK), BF16)],
        compiler_params=pltpu.CompilerParams(
            dimension_semantics=("arbitrary",), vmem_limit_bytes=VMEM_LIMIT),
        name=f"attn_d{dilation}",
    )(q, k, v)


def _out_ffn2_kernel(x_ref, o1_ref, o2_ref, o3_ref, rec_ref,
                     wo_ref, nw_ref, w1_ref, w3_ref, w2_ref, out_ref):
    o_refs = (o1_ref, o2_ref, o3_ref)
    ms = [r[0, PAIRS, :, 0:ATTN_HEADS] for r in o_refs]
    dens = [r[0, PAIRS + 1, :, 0:ATTN_HEADS] for r in o_refs]
    mx = jnp.maximum(jnp.maximum(ms[0], ms[1]), ms[2])
    es = [jnp.exp2(m - mx) for m in ms]
    inv = 1.0 / (es[0] * dens[0] + es[1] * dens[1] + es[2] * dens[2])
    halves = [h for e in es for h in _split_bf16(e * inv)]
    stacked = jnp.concatenate(halves, axis=1)
    kdim = stacked.shape[1]
    src = lax.broadcasted_iota(jnp.int32, (kdim, len(es) * ATTN_WIDTH), 0)
    dst = lax.broadcasted_iota(jnp.int32, (kdim, len(es) * ATTN_WIDTH), 1)
    expand = ((src // (2 * ATTN_HEADS) == dst // ATTN_WIDTH)
              & (src % ATTN_HEADS == (dst % ATTN_WIDTH) // HEAD_DIM)).astype(BF16)
    wide = jnp.dot(stacked, expand, preferred_element_type=F32)
    parts = []
    for hp in range(PAIRS):
        acc = None
        for p in range(len(es)):
            lo_lane = p * ATTN_WIDTH + hp * LANES
            term = wide[:, lo_lane:lo_lane + LANES] * o_refs[p][0, hp]
            acc = term if acc is None else acc + term
        parts.append(acc.astype(BF16))
    mixed = jnp.concatenate(parts + [rec_ref[0]], axis=1)
    x2 = x_ref[0] + jnp.dot(mixed, wo_ref[...], preferred_element_type=F32)
    subs = _row_subtiles(x2.shape[0])
    outs = _swiglu_half_step([x2[r, :] for r in subs], nw_ref[...], w1_ref, w3_ref, w2_ref)
    for r, o in zip(subs, outs):
        out_ref[0, r, :] = o


def _out_ffn2(x3d, os_, rec, wo, nw, w1, w3, w2, tm):
    b, s, d = x3d.shape
    f = w1.shape[1]
    a = ATTN_WIDTH
    row = lambda i, j: (i, j, 0)
    return pl.pallas_call(
        _out_ffn2_kernel,
        grid=(b, s // tm),
        in_specs=[pl.BlockSpec((1, tm, d), row)]
                 + [pl.BlockSpec((1, ATTN_SLABS, tm, LANES), lambda i, j: (i, 0, j, 0))] * 3
                 + [pl.BlockSpec((1, tm, HG_WIDTH), row),
                    _const_spec((a + HG_WIDTH, d)),
                    _const_spec((1, d)),
                    _const_spec((d, f)),
                    _const_spec((d, f)),
                    _const_spec((f, d))],
        out_specs=pl.BlockSpec((1, tm, d), row),
        out_shape=jax.ShapeDtypeStruct((b, s, d), F32),
        compiler_params=pltpu.CompilerParams(
            dimension_semantics=("arbitrary", "arbitrary"), vmem_limit_bytes=VMEM_LIMIT),
        name="out_ffn2",
    )(x3d, *os_, rec, wo, nw, w1, w3, w2)


def _rope_lane_tables(s):
    half = ROPE_DIM // 2
    inv = ROPE_THETA ** (-jnp.arange(0, ROPE_DIM, 2, dtype=F32) / ROPE_DIM)
    ang = jnp.arange(s, dtype=F32)[:, None] * inv[None, :]
    cos, sin = jnp.cos(ang), jnp.sin(ang)
    dim = jnp.arange(LANES) % HEAD_DIM
    idx = dim % half
    c = jnp.where(dim[None, :] < ROPE_DIM, cos[:, idx], 1.0)
    s_lo = jnp.where(dim[None, :] < half, -sin[:, idx], 0.0)
    s_hi = jnp.where((dim[None, :] >= half) & (dim[None, :] < ROPE_DIM), sin[:, idx], 0.0)
    return c.astype(F32), s_lo.astype(F32), s_hi.astype(F32)


def kernel(x, ffn1_norm, ffn1_w1, ffn1_w3, ffn1_w2, mix_norm, w_in, q_norm, k_norm,
           hg_lb_logits, hg_out_norm, w_out, ffn2_norm, ffn2_w1, ffn2_w3, ffn2_w2):
    b, s, d = x.shape
    depth = ffn1_norm.shape[0]
    tm = 512
    cos, s_lo, s_hi = _rope_lane_tables(s)
    head_of = jnp.arange(2 * LANES) // HEAD_DIM
    seg = (head_of[:, None] == head_of[None, :]).astype(BF16)
    for layer in range(depth):
        x1 = _ffn1(x.reshape(b * s, d), ffn1_norm[layer][None, :], ffn1_w1[layer].astype(BF16),
                   ffn1_w3[layer].astype(BF16), ffn1_w2[layer].astype(BF16), tm)
        (q1, k1, v1, q4, k4, v4, q16, k16, v16, rec) = _mix_in(
            x1, mix_norm[layer][None, :], w_in[layer].astype(BF16), seg,
            jnp.tile(q_norm[layer], ATTN_HEADS)[None, :], jnp.tile(k_norm[layer], ATTN_HEADS)[None, :],
            cos, s_lo, s_hi, hg_lb_logits, hg_out_norm[layer][None, :], b, tm, layer)
        os_ = [_attention(q, k, v) for q, k, v in
               ((q1[:, None], k1[:, None], v1[:, None]), (q4, k4, v4), (q16, k16, v16))]
        x = _out_ffn2(x1.reshape(b, s, d), os_, rec, w_out[layer].astype(BF16),
                      ffn2_norm[layer][None, :], ffn2_w1[layer].astype(BF16),
                      ffn2_w3[layer].astype(BF16), ffn2_w2[layer].astype(BF16), tm)
    return x
```

```python
import functools

import jax
import jax.numpy as jnp
from jax import lax
from jax.experimental import pallas as pl
from jax.experimental.pallas import tpu as pltpu

F32 = jnp.float32
BF16 = jnp.bfloat16

EPS = 1e-6
NEG_INF = -1e30
LOG2_E = 1.4426950408889634
HEAD_DIM = 64
ATTN_HEADS = 8
ATTN_WIDTH = ATTN_HEADS * HEAD_DIM
ROPE_DIM = HEAD_DIM // 4
ROPE_THETA = 500000.0
DILATIONS = (1, 4, 16)
HG_HEADS = 4
HG_DIM = 128
HG_CHUNK = 64
HG_WIDTH = HG_HEADS * HG_DIM

LANES = 128
ATTN_BLOCK = 128
PAIRS = ATTN_WIDTH // LANES
ATTN_CHUNK = 8
ATTN_SLABS = PAIRS + 2
MIX_SUBTILES = 2
FFN_SUBTILE_ROWS = 256
FFN1_ROWS = 1024
VMEM_LIMIT = 56 * 1024 * 1024


def _const_spec(shape):
    nd = len(shape)
    return pl.BlockSpec(shape, lambda *_: (0,) * nd, pipeline_mode=pl.Buffered(1))


def _rms_norm_rows(x, w):
    return x * lax.rsqrt(jnp.mean(x * x, axis=-1, keepdims=True) + EPS) * w


def _silu(a):
    return a * (1.0 / (1.0 + jnp.exp(-a)))


def _swiglu_half_step(xs, nw, w1_ref, w3_ref, w2_ref):
    def finish(x, a, b):
        g = (_silu(a) * b).astype(BF16)
        return x + 0.5 * jnp.dot(g, w2_ref[...], preferred_element_type=F32)

    outs, pending = [], None
    for x in xs:
        h = _rms_norm_rows(x, nw).astype(BF16)
        a = jnp.dot(h, w1_ref[...], preferred_element_type=F32)
        b = jnp.dot(h, w3_ref[...], preferred_element_type=F32)
        if pending is not None:
            outs.append(finish(*pending))
        pending = (x, a, b)
    outs.append(finish(*pending))
    return outs


def _split_bf16(x):
    hi = x.astype(BF16)
    return hi, (x - hi.astype(F32)).astype(BF16)


def _row_subtiles(n):
    ts = FFN_SUBTILE_ROWS
    return [slice(i, i + ts) for i in range(0, n, ts)]


def _ffn1_kernel(x_ref, nw_ref, w1_ref, w3_ref, w2_ref, o_ref):
    subs = _row_subtiles(x_ref.shape[0])
    outs = _swiglu_half_step([x_ref[r, :] for r in subs], nw_ref[...], w1_ref, w3_ref, w2_ref)
    for r, o in zip(subs, outs):
        o_ref[r, :] = o


def _ffn1(x2d, nw, w1, w3, w2, tm):
    n, d = x2d.shape
    f = w1.shape[1]
    return pl.pallas_call(
        _ffn1_kernel,
        grid=(n // tm,),
        in_specs=[
            pl.BlockSpec((tm, d), lambda i: (i, 0)),
            _const_spec((1, d)),
            _const_spec((d, f)),
            _const_spec((d, f)),
            _const_spec((f, d)),
        ],
        out_specs=pl.BlockSpec((tm, d), lambda i: (i, 0)),
        out_shape=jax.ShapeDtypeStruct((n, d), F32),
        compiler_params=pltpu.CompilerParams(
            dimension_semantics=("arbitrary",), vmem_limit_bytes=VMEM_LIMIT),
        name="ffn1",
    )(x2d, nw, w1, w3, w2)


def _head_norm_rope(t, seg_ref, w, cos, sin_lo, sin_hi, scale):
    sq = (t * t).astype(BF16)
    seg = seg_ref[...]
    sw = seg.shape[0]
    ms = jnp.concatenate(
        [jnp.dot(sq[:, c:c + sw], seg, preferred_element_type=F32)
         for c in range(0, ATTN_WIDTH, sw)], axis=1) * (1.0 / HEAD_DIM)
    y = t * lax.rsqrt(ms + EPS) * w
    outs = []
    for c in range(PAIRS):
        yc = y[:, c * LANES:(c + 1) * LANES]
        from_hi = pltpu.roll(yc, LANES - ROPE_DIM // 2, axis=1)
        from_lo = pltpu.roll(yc, ROPE_DIM // 2, axis=1)
        outs.append((yc * cos + from_hi * sin_lo + from_lo * sin_hi) * scale)
    return outs


def _store_attn_input(slabs, ti, rows, sub, ts, outs, slab_scr, res4_scr):
    o1, o4, o16 = outs
    lanes = [slice(c * LANES, (c + 1) * LANES) for c in range(PAIRS)]
    for c in range(PAIRS):
        o1[0, rows, lanes[c]] = slabs[c].astype(BF16)
        slab_scr[sub, ti * PAIRS + c] = slabs[c]
    for c in range(PAIRS):
        sl = ti * PAIRS + c
        for r in range(4):
            rows4 = slab_scr[sub, sl, pl.ds(r, ts // 4, stride=4), :]
            o4[0, r, sub * (ts // 4):(sub + 1) * (ts // 4), lanes[c]] = rows4.astype(BF16)
            res4_scr[sub, sl, r] = rows4
    for c in range(PAIRS):
        sl = ti * PAIRS + c
        for r in range(4):
            for a_ in range(4):
                rows16 = res4_scr[sub, sl, r, pl.ds(a_, ts // 16, stride=4), :]
                o16[0, 4 * a_ + r, sub * (ts // 16):(sub + 1) * (ts // 16), lanes[c]] = (
                    rows16.astype(BF16))


def _hgrn_head_pieces(p_at, head, lb, nw, state_scr, seq_start, rec_ref, tm):
    c = HG_CHUNK
    t2 = 2 * c
    ngroup = tm // t2
    base = 3 * ATTN_WIDTH + head * HG_DIM
    col = lambda grp: base + grp * HG_WIDTH
    atts, qds, vs, entering, css, decs = [], [], [], [], [], []

    def masks():
        row = lax.broadcasted_iota(jnp.int32, (t2, HG_DIM), 0)
        return row % c, row < c

    def scan():
        state = jnp.where(seq_start, 0.0, state_scr[head])
        for ch in range(tm // c):
            entering.append(state.astype(BF16))
            state = state * decs[ch] + css[ch]
        state_scr[head] = state

    def score_group(t):
        rows = slice(t * t2, (t + 1) * t2)
        rin, first_chunk = masks()
        cc = lax.broadcasted_iota(jnp.int32, (t2, t2), 1)
        rr = lax.broadcasted_iota(jnp.int32, (t2, t2), 0)
        tril = ((rr // c) == (cc // c)) & ((cc % c) <= (rr % c))
        f = lb + (1.0 - lb) * (1.0 / (1.0 + jnp.exp(-p_at(rows, col(1), HG_DIM))))
        gl = jnp.log(f)
        kk = 1.0 - f
        for sft in (1, 2, 4, 8, 16, 32):
            gl = gl + jnp.where(rin >= sft, pltpu.roll(gl, sft, axis=0), 0.0)
        g_last = jnp.where(first_chunk, gl[c - 1:c, :], gl[t2 - 1:t2, :])
        qd = (p_at(rows, col(0), HG_DIM) * jnp.exp(gl)).astype(BF16)
        kd = (kk * jnp.exp(-gl)).astype(BF16)
        kl = (kk * jnp.exp(g_last - gl)).astype(BF16)
        v = p_at(rows, col(2), HG_DIM).astype(BF16)
        att = lax.dot_general(qd, kd, (((1,), (1,)), ((), ())), preferred_element_type=F32)
        atts.append(jnp.where(tril, att, 0.0).astype(BF16))
        zero = jnp.zeros_like(kl)
        kl2 = jnp.concatenate([jnp.where(first_chunk, kl, zero), jnp.where(first_chunk, zero, kl)],
                              axis=1)
        cs2 = lax.dot_general(v, kl2, (((0,), (0,)), ((), ())), preferred_element_type=F32)
        css.extend([cs2[:, :HG_DIM], cs2[:, HG_DIM:]])
        decs.extend([jnp.exp(gl[c - 1:c, :]), jnp.exp(gl[t2 - 1:t2, :])])
        qds.append(qd)
        vs.append(v)

    def output_group(t):
        rows = slice(t * t2, (t + 1) * t2)
        _, first_chunk = masks()
        intra = jnp.dot(atts[t], vs[t], preferred_element_type=F32)
        states = jnp.concatenate([entering[2 * t], entering[2 * t + 1]], axis=0)
        inter2 = lax.dot_general(qds[t], states, (((1,), (1,)), ((), ())),
                                 preferred_element_type=F32)
        rec = intra + jnp.where(first_chunk, inter2[:, :HG_DIM], inter2[:, HG_DIM:])
        gate = p_at(rows, col(3), HG_DIM)
        rec_ref[0, rows, head * HG_DIM:(head + 1) * HG_DIM] = (
            _rms_norm_rows(rec, nw) * _silu(gate)).astype(BF16)

    def scores():
        for t in range(ngroup):
            score_group(t)
        scan()

    def outputs():
        for t in range(ngroup):
            output_group(t)

    return [scores, outputs]


def _mix_in_kernel(x_ref, nw_ref, w_ref, seg_ref, qw_ref, kw_ref, cos_ref, slo_ref, shi_ref,
                   lbl_ref, hnw_ref,
                   q1_ref, k1_ref, v1_ref, q4_ref, k4_ref, v4_ref, q16_ref, k16_ref, v16_ref,
                   rec_ref, pa_scr, pb_scr, slab_scr, res4_scr, state_scr,
                   *, layer, tiles_per_seq):
    s = pl.program_id(0)
    tm = x_ref.shape[0]
    cols = w_ref.shape[1]
    nsub = slab_scr.shape[0]
    ts = tm // nsub

    @pl.when(s == 0)
    def _():
        pb_scr[...] = jnp.zeros(pb_scr.shape, F32)
        state_scr[...] = jnp.zeros(state_scr.shape, F32)

    def step(write_scr, read_scr):
        a = ATTN_WIDTH
        normed = {}

        pw = ATTN_WIDTH

        def project(sub, c0):
            rows = slice(sub * ts, (sub + 1) * ts)
            if sub not in normed:
                normed[sub] = _rms_norm_rows(x_ref[rows, :], nw_ref[...]).astype(BF16)
            write_scr[rows, c0:c0 + pw] = jnp.dot(normed[sub], w_ref[:, c0:c0 + pw],
                                                  preferred_element_type=F32)

        p_at = lambda rows, c0, width: read_scr[rows, c0:c0 + width]
        outs = ((q1_ref, q4_ref, q16_ref), (k1_ref, k4_ref, k16_ref), (v1_ref, v4_ref, v16_ref))

        def attn_input(sub, ti):
            rows = slice(sub * ts, (sub + 1) * ts)
            if ti == 2:
                slabs = [p_at(rows, 2 * a + c * LANES, LANES) for c in range(PAIRS)]
            else:
                gain, scale = ((qw_ref, LOG2_E * HEAD_DIM ** -0.5), (kw_ref, 1.0))[ti]
                slabs = _head_norm_rope(p_at(rows, ti * a, a), seg_ref, gain[...], cos_ref[rows, :],
                                        slo_ref[rows, :], shi_ref[rows, :], scale)
            _store_attn_input(slabs, ti, rows, sub, ts, outs[ti], slab_scr, res4_scr)

        lg = lbl_ref[...]
        e = jnp.exp(lg - jnp.max(lg, axis=0, keepdims=True))
        lb = jnp.sum(e[0:layer + 1, :], axis=0, keepdims=True) / jnp.sum(e, axis=0, keepdims=True)
        seq_start = (jnp.maximum(s - 1, 0) % tiles_per_seq) == 0

        attn_pieces = [functools.partial(attn_input, sub, ti)
                       for sub in range(nsub) for ti in range(3)]
        hgrn_pieces = []
        for head in range(HG_HEADS):
            hgrn_pieces += _hgrn_head_pieces(p_at, head, lb[:, head * HG_DIM:(head + 1) * HG_DIM],
                                             hnw_ref[...], state_scr, seq_start, rec_ref, tm)
        finishing = []
        for i in range(max(len(attn_pieces), len(hgrn_pieces))):
            finishing += hgrn_pieces[i:i + 1] + attn_pieces[i:i + 1]
        projecting = [functools.partial(project, sub, c0)
                      for sub in range(nsub) for c0 in range(0, cols, pw)]
        done_f = 0
        for i, piece in enumerate(projecting):
            piece()
            upto = (i + 1) * len(finishing) // len(projecting)
            for f_piece in finishing[done_f:upto]:
                f_piece()
            done_f = upto

    pl.when(s % 2 == 0)(lambda: step(pa_scr, pb_scr))
    pl.when(s % 2 == 1)(lambda: step(pb_scr, pa_scr))


def _mix_in(x2d, nw, w_in, seg, qw, kw, cos, slo, shi, lb_logits, hnw, batch, tm, layer):
    n, d = x2d.shape
    s = n // batch
    cols = w_in.shape[1]
    a = ATTN_WIDTH
    ntile = n // tm
    per_seq = s // tm
    ts = tm // MIX_SUBTILES
    done = lambda i: jnp.maximum(i - 1, 0)
    row = lambda i: (done(i) // per_seq, done(i) % per_seq, 0)
    out_specs, out_shape = [], []
    for dil in DILATIONS:
        for _ in range(3):
            if dil == 1:
                out_specs.append(pl.BlockSpec((1, tm, a), row))
                out_shape.append(jax.ShapeDtypeStruct((batch, s, a), BF16))
            else:
                out_specs.append(pl.BlockSpec(
                    (1, dil, tm // dil, a), lambda i: (done(i) // per_seq, 0, done(i) % per_seq, 0)))
                out_shape.append(jax.ShapeDtypeStruct((batch, dil, s // dil, a), BF16))
    out_specs.append(pl.BlockSpec((1, tm, HG_WIDTH), row))
    out_shape.append(jax.ShapeDtypeStruct((batch, s, HG_WIDTH), BF16))
    pos = lambda i: (done(i) % per_seq, 0)
    return pl.pallas_call(
        functools.partial(_mix_in_kernel, layer=layer, tiles_per_seq=per_seq),
        grid=(ntile + 1,),
        in_specs=[
            pl.BlockSpec((tm, d), lambda i: (jnp.minimum(i, ntile - 1), 0)),
            _const_spec((1, d)),
            _const_spec((d, cols)),
            _const_spec(seg.shape),
            _const_spec((1, a)),
            _const_spec((1, a)),
            pl.BlockSpec((tm, LANES), pos),
            pl.BlockSpec((tm, LANES), pos),
            pl.BlockSpec((tm, LANES), pos),
            _const_spec(lb_logits.shape),
            _const_spec((1, HG_DIM)),
        ],
        out_specs=out_specs,
        out_shape=out_shape,
        scratch_shapes=[pltpu.VMEM((tm, cols), F32),
                        pltpu.VMEM((tm, cols), F32),
                        pltpu.VMEM((MIX_SUBTILES, 3 * PAIRS, ts, LANES), F32),
                        pltpu.VMEM((MIX_SUBTILES, 3 * PAIRS, 4, ts // 4, LANES), F32),
                        pltpu.VMEM((HG_HEADS, HG_DIM, HG_DIM), F32)],
        compiler_params=pltpu.CompilerParams(
            dimension_semantics=("arbitrary",), vmem_limit_bytes=VMEM_LIMIT),
        name="mix_in",
    )(x2d, nw, w_in, seg, qw, kw, cos, slo, shi, lb_logits, hnw)


def _attn_kernel(q_ref, k_ref, v_ref, o_ref, p_scr, *, dilation, nblk):
    w = ATTN_BLOCK
    groups = dilation
    lane = lax.broadcasted_iota(jnp.int32, (w, LANES), 1)
    head0 = lane < HEAD_DIM
    qi1 = lax.broadcasted_iota(jnp.int32, (2 * w, w), 0) % w
    kj1 = lax.broadcasted_iota(jnp.int32, (2 * w, w), 1)
    valid_first = kj1 <= qi1
    qi2 = lax.broadcasted_iota(jnp.int32, (2 * w, 2 * w), 0) % w
    kj2 = lax.broadcasted_iota(jnp.int32, (2 * w, 2 * w), 1)
    dist = qi2 + w - kj2
    valid_band = (dist >= 0) & (dist <= w)

    def store_rows(slab, r0, g, val):
        if dilation == 1:
            o_ref[0, slab, pl.ds(r0, w), :] = val
        else:
            o_ref[0, slab, pl.ds(r0 * dilation + g, w, stride=dilation), :] = val

    def score_block(g, qb, r0, rows_k, nk, valid):
        rows_q = pl.ds(r0, w)
        m_acc = jnp.zeros((w, LANES), F32)
        for hp in range(PAIRS):
            lanes = slice(hp * LANES, (hp + 1) * LANES)
            q2 = q_ref[0, g, rows_q, lanes]
            zero = jnp.zeros_like(q2)
            qq = jnp.concatenate([jnp.where(head0, q2, zero), jnp.where(head0, zero, q2)], axis=0)
            s = lax.dot_general(qq, k_ref[0, g, rows_k, lanes], (((1,), (1,)), ((), ())),
                                preferred_element_type=F32)
            s = jnp.where(valid, s, NEG_INF)
            m = jnp.max(s, axis=1, keepdims=True)
            p_scr[g * nblk + qb, hp, :, 0:nk] = jnp.exp2(s - m).astype(BF16)
            for h in range(2):
                m_acc = jnp.where(lane == 2 * hp + h, m[h * w:(h + 1) * w], m_acc)
        store_rows(PAIRS, r0, g, m_acc)

    def value_block(g, qb, r0, rows_k, nk):
        d_acc = jnp.zeros((w, LANES), F32)
        ones = jnp.ones((nk, LANES), BF16)
        for hp in range(PAIRS):
            lanes = slice(hp * LANES, (hp + 1) * LANES)
            vx = jnp.concatenate([v_ref[0, g, rows_k, lanes], ones], axis=1)
            r = jnp.dot(p_scr[g * nblk + qb, hp, :, 0:nk], vx, preferred_element_type=F32)
            store_rows(hp, r0, g, jnp.where(head0, r[:w, :LANES], r[w:, :LANES]))
            for h in range(2):
                d_acc = jnp.where(lane == 2 * hp + h, r[h * w:(h + 1) * w, LANES:], d_acc)
        store_rows(PAIRS + 1, r0, g, d_acc)

    def run_chunk(blocks):
        args = []
        for g, qb in blocks:
            if isinstance(qb, int) and qb == 0:
                args.append((g, 0, 0, pl.ds(0, w), w, valid_first))
            else:
                r0 = qb * w if isinstance(qb, int) else pl.multiple_of(qb * w, w)
                args.append((g, qb, r0, pl.ds(r0 - w, 2 * w), 2 * w, valid_band))
        for g, qb, r0, rows_k, nk, valid in args:
            score_block(g, qb, r0, rows_k, nk, valid)
        for g, qb, r0, rows_k, nk, _ in args:
            value_block(g, qb, r0, rows_k, nk)

    def loop_chunks(lo, hi, blocks_of):
        def body(i, carry):
            run_chunk(blocks_of(i))
            return carry
        lax.fori_loop(lo, hi, body, 0)

    c = ATTN_CHUNK
    if nblk <= c:
        per = min(c // nblk, groups)
        loop_chunks(0, groups // per,
                    lambda i: [(i * per + g, qb) for g in range(per) for qb in range(nblk)])
    else:
        run_chunk([(0, qb) for qb in range(c)])
        loop_chunks(1, nblk // c, lambda i: [(0, i * c + j) for j in range(c)])


def _attention(q, k, v):
    b, dilation, l, a = q.shape
    s = l * dilation
    nblk = l // ATTN_BLOCK
    spec = pl.BlockSpec((1, dilation, l, a), lambda i: (i, 0, 0, 0))
    return pl.pallas_call(
        functools.partial(_attn_kernel, dilation=dilation, nblk=nblk),
        grid=(b,),
        in_specs=[spec, spec, spec],
        out_specs=pl.BlockSpec((1, ATTN_SLABS, s, LANES), lambda i: (i, 0, 0, 0)),
        out_shape=jax.ShapeDtypeStruct((b, ATTN_SLABS, s, LANES), F32),
        scratch_shapes=[pltpu.VMEM((dilation * nblk, PAIRS, 2 * ATTN_BLOCK, 2 * ATTN_BLOCK), BF16)],
        compiler_params=pltpu.CompilerParams(
            dimension_semantics=("arbitrary",), vmem_limit_bytes=VMEM_LIMIT),
        name=f"attn_d{dilation}",
    )(q, k, v)


def _out_ffn2_kernel(x_ref, o1_ref, o2_ref, o3_ref, rec_ref,
                     wo_ref, nw_ref, w1_ref, w3_ref, w2_ref, out_ref):
    o_refs = (o1_ref, o2_ref, o3_ref)
    ms = [r[0, PAIRS, :, 0:ATTN_HEADS] for r in o_refs]
    dens = [r[0, PAIRS + 1, :, 0:ATTN_HEADS] for r in o_refs]
    mx = jnp.maximum(jnp.maximum(ms[0], ms[1]), ms[2])
    es = [jnp.exp2(m - mx) for m in ms]
    inv = 1.0 / (es[0] * dens[0] + es[1] * dens[1] + es[2] * dens[2])
    halves = [h for e in es for h in _split_bf16(e * inv)]
    stacked = jnp.concatenate(halves, axis=1)
    kdim = stacked.shape[1]
    src = lax.broadcasted_iota(jnp.int32, (kdim, len(es) * ATTN_WIDTH), 0)
    dst = lax.broadcasted_iota(jnp.int32, (kdim, len(es) * ATTN_WIDTH), 1)
    expand = ((src // (2 * ATTN_HEADS) == dst // ATTN_WIDTH)
              & (src % ATTN_HEADS == (dst % ATTN_WIDTH) // HEAD_DIM)).astype(BF16)
    wide = jnp.dot(stacked, expand, preferred_element_type=F32)
    parts = []
    for hp in range(PAIRS):
        acc = None
        for p in range(len(es)):
            lo_lane = p * ATTN_WIDTH + hp * LANES
            term = wide[:, lo_lane:lo_lane + LANES] * o_refs[p][0, hp]
            acc = term if acc is None else acc + term
        parts.append(acc.astype(BF16))
    mixed = jnp.concatenate(parts + [rec_ref[0]], axis=1)
    x2 = x_ref[0] + jnp.dot(mixed, wo_ref[...], preferred_element_type=F32)
    subs = _row_subtiles(x2.shape[0])
    outs = _swiglu_half_step([x2[r, :] for r in subs], nw_ref[...], w1_ref, w3_ref, w2_ref)
    for r, o in zip(subs, outs):
        out_ref[0, r, :] = o


def _out_ffn2(x3d, os_, rec, wo, nw, w1, w3, w2, tm):
    b, s, d = x3d.shape
    f = w1.shape[1]
    a = ATTN_WIDTH
    row = lambda i, j: (i, j, 0)
    return pl.pallas_call(
        _out_ffn2_kernel,
        grid=(b, s // tm),
        in_specs=[pl.BlockSpec((1, tm, d), row)]
                 + [pl.BlockSpec((1, ATTN_SLABS, tm, LANES), lambda i, j: (i, 0, j, 0))] * 3
                 + [pl.BlockSpec((1, tm, HG_WIDTH), row),
                    _const_spec((a + HG_WIDTH, d)),
                    _const_spec((1, d)),
                    _const_spec((d, f)),
                    _const_spec((d, f)),
                    _const_spec((f, d))],
        out_specs=pl.BlockSpec((1, tm, d), row),
        out_shape=jax.ShapeDtypeStruct((b, s, d), F32),
        compiler_params=pltpu.CompilerParams(
            dimension_semantics=("arbitrary", "arbitrary"), vmem_limit_bytes=VMEM_LIMIT),
        name="out_ffn2",
    )(x3d, *os_, rec, wo, nw, w1, w3, w2)


def _rope_lane_tables(s):
    half = ROPE_DIM // 2
    inv = ROPE_THETA ** (-jnp.arange(0, ROPE_DIM, 2, dtype=F32) / ROPE_DIM)
    ang = jnp.arange(s, dtype=F32)[:, None] * inv[None, :]
    cos, sin = jnp.cos(ang), jnp.sin(ang)
    dim = jnp.arange(LANES) % HEAD_DIM
    idx = dim % half
    c = jnp.where(dim[None, :] < ROPE_DIM, cos[:, idx], 1.0)
    s_lo = jnp.where(dim[None, :] < half, -sin[:, idx], 0.0)
    s_hi = jnp.where((dim[None, :] >= half) & (dim[None, :] < ROPE_DIM), sin[:, idx], 0.0)
    return c.astype(F32), s_lo.astype(F32), s_hi.astype(F32)


def kernel(x, ffn1_norm, ffn1_w1, ffn1_w3, ffn1_w2, mix_norm, w_in, q_norm, k_norm,
           hg_lb_logits, hg_out_norm, w_out, ffn2_norm, ffn2_w1, ffn2_w3, ffn2_w2):
    b, s, d = x.shape
    depth = ffn1_norm.shape[0]
    tm = 512
    cos, s_lo, s_hi = _rope_lane_tables(s)
    head_of = jnp.arange(2 * LANES) // HEAD_DIM
    seg = (head_of[:, None] == head_of[None, :]).astype(BF16)
    for layer in range(depth):
        x1 = _ffn1(x.reshape(b * s, d), ffn1_norm[layer][None, :], ffn1_w1[layer].astype(BF16),
                   ffn1_w3[layer].astype(BF16), ffn1_w2[layer].astype(BF16), FFN1_ROWS)
        (q1, k1, v1, q4, k4, v4, q16, k16, v16, rec) = _mix_in(
            x1, mix_norm[layer][None, :], w_in[layer].astype(BF16), seg,
            jnp.tile(q_norm[layer], ATTN_HEADS)[None, :], jnp.tile(k_norm[layer], ATTN_HEADS)[None, :],
            cos, s_lo, s_hi, hg_lb_logits, hg_out_norm[layer][None, :], b, tm, layer)
        os_ = [_attention(q, k, v) for q, k, v in
               ((q1[:, None], k1[:, None], v1[:, None]), (q4, k4, v4), (q16, k16, v16))]
        x = _out_ffn2(x1.reshape(b, s, d), os_, rec, w_out[layer].astype(BF16),
                      ffn2_norm[layer][None, :], ffn2_w1[layer].astype(BF16),
                      ffn2_w3[layer].astype(BF16), ffn2_w2[layer].astype(BF16), tm)
    return x
```

```python
import functools

import jax
import jax.numpy as jnp
from jax import lax
from jax.experimental import pallas as pl
from jax.experimental.pallas import tpu as pltpu

F32 = jnp.float32
BF16 = jnp.bfloat16

EPS = 1e-6
NEG_INF = -1e30
LOG2_E = 1.4426950408889634
HEAD_DIM = 64
ATTN_HEADS = 8
ATTN_WIDTH = ATTN_HEADS * HEAD_DIM
ROPE_DIM = HEAD_DIM // 4
ROPE_THETA = 500000.0
DILATIONS = (1, 4, 16)
HG_HEADS = 4
HG_DIM = 128
HG_CHUNK = 64
HG_WIDTH = HG_HEADS * HG_DIM

LANES = 128
BF16_TILE_ROWS = 16
ATTN_BLOCK = 128
PAIRS = ATTN_WIDTH // LANES
ATTN_CHUNK = 8
ATTN_SLABS = PAIRS + 2
MIX_SUBTILES = 2
FFN_SUBTILE_ROWS = 256
FFN1_ROWS = 1024
VMEM_LIMIT = 56 * 1024 * 1024


def _const_spec(shape):
    nd = len(shape)
    return pl.BlockSpec(shape, lambda *_: (0,) * nd, pipeline_mode=pl.Buffered(1))


def _rms_norm_rows(x, w):
    return x * lax.rsqrt(jnp.mean(x * x, axis=-1, keepdims=True) + EPS) * w


def _silu(a):
    return a * (1.0 / (1.0 + jnp.exp(-a)))


def _swiglu_half_step(xs, nw, w1_ref, w3_ref, w2_ref):
    def finish(x, a, b):
        g = (_silu(a) * b).astype(BF16)
        return x + 0.5 * jnp.dot(g, w2_ref[...], preferred_element_type=F32)

    outs, pending = [], None
    for x in xs:
        h = _rms_norm_rows(x, nw).astype(BF16)
        a = jnp.dot(h, w1_ref[...], preferred_element_type=F32)
        b = jnp.dot(h, w3_ref[...], preferred_element_type=F32)
        if pending is not None:
            outs.append(finish(*pending))
        pending = (x, a, b)
    outs.append(finish(*pending))
    return outs


def _split_bf16(x):
    hi = x.astype(BF16)
    return hi, (x - hi.astype(F32)).astype(BF16)


def _row_subtiles(n):
    ts = FFN_SUBTILE_ROWS
    return [slice(i, i + ts) for i in range(0, n, ts)]


def _ffn1_kernel(*refs, cast_steps, nsteps):
    ncast = len(cast_steps)
    x_ref, nw_ref, w1_ref, w3_ref, w2_ref = refs[:5]
    cast_in = refs[5:5 + ncast]
    o_ref = refs[5 + ncast]
    cast_out = refs[6 + ncast:]
    step = pl.program_id(0)
    for src, dst, active in zip(cast_in, cast_out, cast_steps):
        if active == nsteps:
            dst[...] = src[...].astype(BF16)
        else:
            @pl.when(step < active)
            def _(src=src, dst=dst):
                dst[...] = src[...].astype(BF16)
    subs = _row_subtiles(x_ref.shape[0])
    outs = _swiglu_half_step([x_ref[r, :] for r in subs], nw_ref[...], w1_ref, w3_ref, w2_ref)
    for r, o in zip(subs, outs):
        o_ref[r, :] = o


def _cast_chunk_rows(rows, nsteps):
    for cr in range(BF16_TILE_ROWS, rows + 1, BF16_TILE_ROWS):
        if rows % cr == 0 and rows // cr <= nsteps:
            return cr
    raise ValueError(f"no bf16-aligned row chunk for {rows} rows in {nsteps} steps")


def _ffn1(x2d, nw, w1, w3, w2, tm, later_weights):
    n, d = x2d.shape
    f = w1.shape[1]
    nsteps = n // tm
    cast_specs, cast_shapes, cast_steps = [], [], []
    for w in later_weights:
        cr = _cast_chunk_rows(w.shape[0], nsteps)
        active = w.shape[0] // cr
        cast_specs.append(pl.BlockSpec((cr, w.shape[1]),
                                       lambda i, last=active - 1: (jnp.minimum(i, last), 0)))
        cast_shapes.append(jax.ShapeDtypeStruct(w.shape, BF16))
        cast_steps.append(active)
    outs = pl.pallas_call(
        functools.partial(_ffn1_kernel, cast_steps=tuple(cast_steps), nsteps=nsteps),
        grid=(nsteps,),
        in_specs=[
            pl.BlockSpec((tm, d), lambda i: (i, 0)),
            _const_spec((1, d)),
            _const_spec((d, f)),
            _const_spec((d, f)),
            _const_spec((f, d)),
        ] + cast_specs,
        out_specs=[pl.BlockSpec((tm, d), lambda i: (i, 0))] + cast_specs,
        out_shape=[jax.ShapeDtypeStruct((n, d), F32)] + cast_shapes,
        compiler_params=pltpu.CompilerParams(
            dimension_semantics=("arbitrary",), vmem_limit_bytes=VMEM_LIMIT),
        name="ffn1",
    )(x2d, nw, w1, w3, w2, *later_weights)
    return outs[0], outs[1:]


def _head_norm_rope(t, seg_ref, w, cos, sin_lo, sin_hi, scale):
    sq = (t * t).astype(BF16)
    seg = seg_ref[...]
    sw = seg.shape[0]
    ms = jnp.concatenate(
        [jnp.dot(sq[:, c:c + sw], seg, preferred_element_type=F32)
         for c in range(0, ATTN_WIDTH, sw)], axis=1) * (1.0 / HEAD_DIM)
    y = t * lax.rsqrt(ms + EPS) * w
    outs = []
    for c in range(PAIRS):
        yc = y[:, c * LANES:(c + 1) * LANES]
        from_hi = pltpu.roll(yc, LANES - ROPE_DIM // 2, axis=1)
        from_lo = pltpu.roll(yc, ROPE_DIM // 2, axis=1)
        outs.append((yc * cos + from_hi * sin_lo + from_lo * sin_hi) * scale)
    return outs


def _store_attn_input(slabs, ti, rows, sub, ts, outs, slab_scr, res4_scr):
    o1, o4, o16 = outs
    lanes = [slice(c * LANES, (c + 1) * LANES) for c in range(PAIRS)]
    for c in range(PAIRS):
        o1[0, rows, lanes[c]] = slabs[c].astype(BF16)
        slab_scr[sub, ti * PAIRS + c] = slabs[c]
    for c in range(PAIRS):
        sl = ti * PAIRS + c
        for r in range(4):
            rows4 = slab_scr[sub, sl, pl.ds(r, ts // 4, stride=4), :]
            o4[0, r, sub * (ts // 4):(sub + 1) * (ts // 4), lanes[c]] = rows4.astype(BF16)
            res4_scr[sub, sl, r] = rows4
    for c in range(PAIRS):
        sl = ti * PAIRS + c
        for r in range(4):
            for a_ in range(4):
                rows16 = res4_scr[sub, sl, r, pl.ds(a_, ts // 16, stride=4), :]
                o16[0, 4 * a_ + r, sub * (ts // 16):(sub + 1) * (ts // 16), lanes[c]] = (
                    rows16.astype(BF16))


def _hgrn_head_pieces(p_at, head, lb, nw, state_scr, seq_start, rec_ref, tm):
    c = HG_CHUNK
    t2 = 2 * c
    ngroup = tm // t2
    base = 3 * ATTN_WIDTH + head * HG_DIM
    col = lambda grp: base + grp * HG_WIDTH
    atts, qds, vs, entering, css, decs = [], [], [], [], [], []

    def masks():
        row = lax.broadcasted_iota(jnp.int32, (t2, HG_DIM), 0)
        return row % c, row < c

    def scan():
        state = jnp.where(seq_start, 0.0, state_scr[head])
        for ch in range(tm // c):
            entering.append(state.astype(BF16))
            state = state * decs[ch] + css[ch]
        state_scr[head] = state

    def score_group(t):
        rows = slice(t * t2, (t + 1) * t2)
        rin, first_chunk = masks()
        cc = lax.broadcasted_iota(jnp.int32, (t2, t2), 1)
        rr = lax.broadcasted_iota(jnp.int32, (t2, t2), 0)
        tril = ((rr // c) == (cc // c)) & ((cc % c) <= (rr % c))
        f = lb + (1.0 - lb) * (1.0 / (1.0 + jnp.exp(-p_at(rows, col(1), HG_DIM))))
        gl = jnp.log(f)
        kk = 1.0 - f
        for sft in (1, 2, 4, 8, 16, 32):
            gl = gl + jnp.where(rin >= sft, pltpu.roll(gl, sft, axis=0), 0.0)
        g_last = jnp.where(first_chunk, gl[c - 1:c, :], gl[t2 - 1:t2, :])
        qd = (p_at(rows, col(0), HG_DIM) * jnp.exp(gl)).astype(BF16)
        kd = (kk * jnp.exp(-gl)).astype(BF16)
        kl = (kk * jnp.exp(g_last - gl)).astype(BF16)
        v = p_at(rows, col(2), HG_DIM).astype(BF16)
        att = lax.dot_general(qd, kd, (((1,), (1,)), ((), ())), preferred_element_type=F32)
        atts.append(jnp.where(tril, att, 0.0).astype(BF16))
        zero = jnp.zeros_like(kl)
        kl2 = jnp.concatenate([jnp.where(first_chunk, kl, zero), jnp.where(first_chunk, zero, kl)],
                              axis=1)
        cs2 = lax.dot_general(v, kl2, (((0,), (0,)), ((), ())), preferred_element_type=F32)
        css.extend([cs2[:, :HG_DIM], cs2[:, HG_DIM:]])
        decs.extend([jnp.exp(gl[c - 1:c, :]), jnp.exp(gl[t2 - 1:t2, :])])
        qds.append(qd)
        vs.append(v)

    def output_group(t):
        rows = slice(t * t2, (t + 1) * t2)
        _, first_chunk = masks()
        intra = jnp.dot(atts[t], vs[t], preferred_element_type=F32)
        states = jnp.concatenate([entering[2 * t], entering[2 * t + 1]], axis=0)
        inter2 = lax.dot_general(qds[t], states, (((1,), (1,)), ((), ())),
                                 preferred_element_type=F32)
        rec = intra + jnp.where(first_chunk, inter2[:, :HG_DIM], inter2[:, HG_DIM:])
        gate = p_at(rows, col(3), HG_DIM)
        rec_ref[0, rows, head * HG_DIM:(head + 1) * HG_DIM] = (
            _rms_norm_rows(rec, nw) * _silu(gate)).astype(BF16)

    def scores():
        for t in range(ngroup):
            score_group(t)
        scan()

    def outputs():
        for t in range(ngroup):
            output_group(t)

    return [scores, outputs]


def _mix_in_kernel(x_ref, nw_ref, w_ref, seg_ref, qw_ref, kw_ref, cos_ref, slo_ref, shi_ref,
                   lbl_ref, hnw_ref,
                   q1_ref, k1_ref, v1_ref, q4_ref, k4_ref, v4_ref, q16_ref, k16_ref, v16_ref,
                   rec_ref, pa_scr, pb_scr, slab_scr, res4_scr, state_scr,
                   *, layer, tiles_per_seq):
    s = pl.program_id(0)
    tm = x_ref.shape[0]
    cols = w_ref.shape[1]
    nsub = slab_scr.shape[0]
    ts = tm // nsub

    @pl.when(s == 0)
    def _():
        pb_scr[...] = jnp.zeros(pb_scr.shape, F32)
        state_scr[...] = jnp.zeros(state_scr.shape, F32)

    def step(write_scr, read_scr):
        a = ATTN_WIDTH
        normed = {}

        pw = ATTN_WIDTH

        def project(sub, c0):
            rows = slice(sub * ts, (sub + 1) * ts)
            if sub not in normed:
                normed[sub] = _rms_norm_rows(x_ref[rows, :], nw_ref[...]).astype(BF16)
            write_scr[rows, c0:c0 + pw] = jnp.dot(normed[sub], w_ref[:, c0:c0 + pw],
                                                  preferred_element_type=F32)

        p_at = lambda rows, c0, width: read_scr[rows, c0:c0 + width]
        outs = ((q1_ref, q4_ref, q16_ref), (k1_ref, k4_ref, k16_ref), (v1_ref, v4_ref, v16_ref))

        def attn_input(sub, ti):
            rows = slice(sub * ts, (sub + 1) * ts)
            if ti == 2:
                slabs = [p_at(rows, 2 * a + c * LANES, LANES) for c in range(PAIRS)]
            else:
                gain, scale = ((qw_ref, LOG2_E * HEAD_DIM ** -0.5), (kw_ref, 1.0))[ti]
                slabs = _head_norm_rope(p_at(rows, ti * a, a), seg_ref, gain[...], cos_ref[rows, :],
                                        slo_ref[rows, :], shi_ref[rows, :], scale)
            _store_attn_input(slabs, ti, rows, sub, ts, outs[ti], slab_scr, res4_scr)

        lg = lbl_ref[...]
        e = jnp.exp(lg - jnp.max(lg, axis=0, keepdims=True))
        lb = jnp.sum(e[0:layer + 1, :], axis=0, keepdims=True) / jnp.sum(e, axis=0, keepdims=True)
        seq_start = (jnp.maximum(s - 1, 0) % tiles_per_seq) == 0

        attn_pieces = [functools.partial(attn_input, sub, ti)
                       for sub in range(nsub) for ti in range(3)]
        hgrn_pieces = []
        for head in range(HG_HEADS):
            hgrn_pieces += _hgrn_head_pieces(p_at, head, lb[:, head * HG_DIM:(head + 1) * HG_DIM],
                                             hnw_ref[...], state_scr, seq_start, rec_ref, tm)
        finishing = []
        for i in range(max(len(attn_pieces), len(hgrn_pieces))):
            finishing += hgrn_pieces[i:i + 1] + attn_pieces[i:i + 1]
        projecting = [functools.partial(project, sub, c0)
                      for sub in range(nsub) for c0 in range(0, cols, pw)]
        done_f = 0
        for i, piece in enumerate(projecting):
            piece()
            upto = (i + 1) * len(finishing) // len(projecting)
            for f_piece in finishing[done_f:upto]:
                f_piece()
            done_f = upto

    pl.when(s % 2 == 0)(lambda: step(pa_scr, pb_scr))
    pl.when(s % 2 == 1)(lambda: step(pb_scr, pa_scr))


def _mix_in(x2d, nw, w_in, seg, qw, kw, cos, slo, shi, lb_logits, hnw, batch, tm, layer):
    n, d = x2d.shape
    s = n // batch
    cols = w_in.shape[1]
    a = ATTN_WIDTH
    ntile = n // tm
    per_seq = s // tm
    ts = tm // MIX_SUBTILES
    done = lambda i: jnp.maximum(i - 1, 0)
    row = lambda i: (done(i) // per_seq, done(i) % per_seq, 0)
    out_specs, out_shape = [], []
    for dil in DILATIONS:
        for _ in range(3):
            if dil == 1:
                out_specs.append(pl.BlockSpec((1, tm, a), row))
                out_shape.append(jax.ShapeDtypeStruct((batch, s, a), BF16))
            else:
                out_specs.append(pl.BlockSpec(
                    (1, dil, tm // dil, a), lambda i: (done(i) // per_seq, 0, done(i) % per_seq, 0)))
                out_shape.append(jax.ShapeDtypeStruct((batch, dil, s // dil, a), BF16))
    out_specs.append(pl.BlockSpec((1, tm, HG_WIDTH), row))
    out_shape.append(jax.ShapeDtypeStruct((batch, s, HG_WIDTH), BF16))
    pos = lambda i: (done(i) % per_seq, 0)
    return pl.pallas_call(
        functools.partial(_mix_in_kernel, layer=layer, tiles_per_seq=per_seq),
        grid=(ntile + 1,),
        in_specs=[
            pl.BlockSpec((tm, d), lambda i: (jnp.minimum(i, ntile - 1), 0)),
            _const_spec((1, d)),
            _const_spec((d, cols)),
            _const_spec(seg.shape),
            _const_spec((1, a)),
            _const_spec((1, a)),
            pl.BlockSpec((tm, LANES), pos),
            pl.BlockSpec((tm, LANES), pos),
            pl.BlockSpec((tm, LANES), pos),
            _const_spec(lb_logits.shape),
            _const_spec((1, HG_DIM)),
        ],
        out_specs=out_specs,
        out_shape=out_shape,
        scratch_shapes=[pltpu.VMEM((tm, cols), F32),
                        pltpu.VMEM((tm, cols), F32),
                        pltpu.VMEM((MIX_SUBTILES, 3 * PAIRS, ts, LANES), F32),
                        pltpu.VMEM((MIX_SUBTILES, 3 * PAIRS, 4, ts // 4, LANES), F32),
                        pltpu.VMEM((HG_HEADS, HG_DIM, HG_DIM), F32)],
        compiler_params=pltpu.CompilerParams(
            dimension_semantics=("arbitrary",), vmem_limit_bytes=VMEM_LIMIT),
        name="mix_in",
    )(x2d, nw, w_in, seg, qw, kw, cos, slo, shi, lb_logits, hnw)


def _attn_kernel(q_ref, k_ref, v_ref, o_ref, p_scr, *, dilation, nblk):
    w = ATTN_BLOCK
    groups = dilation
    lane = lax.broadcasted_iota(jnp.int32, (w, LANES), 1)
    head0 = lane < HEAD_DIM
    qi1 = lax.broadcasted_iota(jnp.int32, (2 * w, w), 0) % w
    kj1 = lax.broadcasted_iota(jnp.int32, (2 * w, w), 1)
    valid_first = kj1 <= qi1
    qi2 = lax.broadcasted_iota(jnp.int32, (2 * w, 2 * w), 0) % w
    kj2 = lax.broadcasted_iota(jnp.int32, (2 * w, 2 * w), 1)
    dist = qi2 + w - kj2
    valid_band = (dist >= 0) & (dist <= w)

    def store_rows(slab, r0, g, val):
        if dilation == 1:
            o_ref[0, slab, pl.ds(r0, w), :] = val
        else:
            o_ref[0, slab, pl.ds(r0 * dilation + g, w, stride=dilation), :] = val

    def score_block(g, qb, r0, rows_k, nk, valid):
        rows_q = pl.ds(r0, w)
        m_acc = jnp.zeros((w, LANES), F32)
        for hp in range(PAIRS):
            lanes = slice(hp * LANES, (hp + 1) * LANES)
            q2 = q_ref[0, g, rows_q, lanes]
            zero = jnp.zeros_like(q2)
            qq = jnp.concatenate([jnp.where(head0, q2, zero), jnp.where(head0, zero, q2)], axis=0)
            s = lax.dot_general(qq, k_ref[0, g, rows_k, lanes], (((1,), (1,)), ((), ())),
                                preferred_element_type=F32)
            s = jnp.where(valid, s, NEG_INF)
            m = jnp.max(s, axis=1, keepdims=True)
            p_scr[g * nblk + qb, hp, :, 0:nk] = jnp.exp2(s - m).astype(BF16)
            for h in range(2):
                m_acc = jnp.where(lane == 2 * hp + h, m[h * w:(h + 1) * w], m_acc)
        store_rows(PAIRS, r0, g, m_acc)

    def value_block(g, qb, r0, rows_k, nk):
        d_acc = jnp.zeros((w, LANES), F32)
        ones = jnp.ones((nk, LANES), BF16)
        for hp in range(PAIRS):
            lanes = slice(hp * LANES, (hp + 1) * LANES)
            vx = jnp.concatenate([v_ref[0, g, rows_k, lanes], ones], axis=1)
            r = jnp.dot(p_scr[g * nblk + qb, hp, :, 0:nk], vx, preferred_element_type=F32)
            store_rows(hp, r0, g, jnp.where(head0, r[:w, :LANES], r[w:, :LANES]))
            for h in range(2):
                d_acc = jnp.where(lane == 2 * hp + h, r[h * w:(h + 1) * w, LANES:], d_acc)
        store_rows(PAIRS + 1, r0, g, d_acc)

    def run_chunk(blocks):
        args = []
        for g, qb in blocks:
            if isinstance(qb, int) and qb == 0:
                args.append((g, 0, 0, pl.ds(0, w), w, valid_first))
            else:
                r0 = qb * w if isinstance(qb, int) else pl.multiple_of(qb * w, w)
                args.append((g, qb, r0, pl.ds(r0 - w, 2 * w), 2 * w, valid_band))
        for g, qb, r0, rows_k, nk, valid in args:
            score_block(g, qb, r0, rows_k, nk, valid)
        for g, qb, r0, rows_k, nk, _ in args:
            value_block(g, qb, r0, rows_k, nk)

    def loop_chunks(lo, hi, blocks_of):
        def body(i, carry):
            run_chunk(blocks_of(i))
            return carry
        lax.fori_loop(lo, hi, body, 0)

    c = ATTN_CHUNK
    if nblk <= c:
        per = min(c // nblk, groups)
        loop_chunks(0, groups // per,
                    lambda i: [(i * per + g, qb) for g in range(per) for qb in range(nblk)])
    else:
        run_chunk([(0, qb) for qb in range(c)])
        loop_chunks(1, nblk // c, lambda i: [(0, i * c + j) for j in range(c)])


def _attention(q, k, v):
    b, dilation, l, a = q.shape
    s = l * dilation
    nblk = l // ATTN_BLOCK
    spec = pl.BlockSpec((1, dilation, l, a), lambda i: (i, 0, 0, 0))
    return pl.pallas_call(
        functools.partial(_attn_kernel, dilation=dilation, nblk=nblk),
        grid=(b,),
        in_specs=[spec, spec, spec],
        out_specs=pl.BlockSpec((1, ATTN_SLABS, s, LANES), lambda i: (i, 0, 0, 0)),
        out_shape=jax.ShapeDtypeStruct((b, ATTN_SLABS, s, LANES), F32),
        scratch_shapes=[pltpu.VMEM((dilation * nblk, PAIRS, 2 * ATTN_BLOCK, 2 * ATTN_BLOCK), BF16)],
        compiler_params=pltpu.CompilerParams(
            dimension_semantics=("arbitrary",), vmem_limit_bytes=VMEM_LIMIT),
        name=f"attn_d{dilation}",
    )(q, k, v)


def _out_ffn2_kernel(x_ref, o1_ref, o2_ref, o3_ref, rec_ref,
                     wo_ref, nw_ref, w1_ref, w3_ref, w2_ref, out_ref):
    o_refs = (o1_ref, o2_ref, o3_ref)
    ms = [r[0, PAIRS, :, 0:ATTN_HEADS] for r in o_refs]
    dens = [r[0, PAIRS + 1, :, 0:ATTN_HEADS] for r in o_refs]
    mx = jnp.maximum(jnp.maximum(ms[0], ms[1]), ms[2])
    es = [jnp.exp2(m - mx) for m in ms]
    inv = 1.0 / (es[0] * dens[0] + es[1] * dens[1] + es[2] * dens[2])
    halves = [h for e in es for h in _split_bf16(e * inv)]
    stacked = jnp.concatenate(halves, axis=1)
    kdim = stacked.shape[1]
    src = lax.broadcasted_iota(jnp.int32, (kdim, len(es) * ATTN_WIDTH), 0)
    dst = lax.broadcasted_iota(jnp.int32, (kdim, len(es) * ATTN_WIDTH), 1)
    expand = ((src // (2 * ATTN_HEADS) == dst // ATTN_WIDTH)
              & (src % ATTN_HEADS == (dst % ATTN_WIDTH) // HEAD_DIM)).astype(BF16)
    wide = jnp.dot(stacked, expand, preferred_element_type=F32)
    parts = []
    for hp in range(PAIRS):
        acc = None
        for p in range(len(es)):
            lo_lane = p * ATTN_WIDTH + hp * LANES
            term = wide[:, lo_lane:lo_lane + LANES] * o_refs[p][0, hp]
            acc = term if acc is None else acc + term
        parts.append(acc.astype(BF16))
    mixed = jnp.concatenate(parts + [rec_ref[0]], axis=1)
    x2 = x_ref[0] + jnp.dot(mixed, wo_ref[...], preferred_element_type=F32)
    subs = _row_subtiles(x2.shape[0])
    outs = _swiglu_half_step([x2[r, :] for r in subs], nw_ref[...], w1_ref, w3_ref, w2_ref)
    for r, o in zip(subs, outs):
        out_ref[0, r, :] = o


def _out_ffn2(x3d, os_, rec, wo, nw, w1, w3, w2, tm):
    b, s, d = x3d.shape
    f = w1.shape[1]
    a = ATTN_WIDTH
    row = lambda i, j: (i, j, 0)
    return pl.pallas_call(
        _out_ffn2_kernel,
        grid=(b, s // tm),
        in_specs=[pl.BlockSpec((1, tm, d), row)]
                 + [pl.BlockSpec((1, ATTN_SLABS, tm, LANES), lambda i, j: (i, 0, j, 0))] * 3
                 + [pl.BlockSpec((1, tm, HG_WIDTH), row),
                    _const_spec((a + HG_WIDTH, d)),
                    _const_spec((1, d)),
                    _const_spec((d, f)),
                    _const_spec((d, f)),
                    _const_spec((f, d))],
        out_specs=pl.BlockSpec((1, tm, d), row),
        out_shape=jax.ShapeDtypeStruct((b, s, d), F32),
        compiler_params=pltpu.CompilerParams(
            dimension_semantics=("arbitrary", "arbitrary"), vmem_limit_bytes=VMEM_LIMIT),
        name="out_ffn2",
    )(x3d, *os_, rec, wo, nw, w1, w3, w2)


def _rope_lane_tables(s):
    half = ROPE_DIM // 2
    inv = ROPE_THETA ** (-jnp.arange(0, ROPE_DIM, 2, dtype=F32) / ROPE_DIM)
    ang = jnp.arange(s, dtype=F32)[:, None] * inv[None, :]
    cos, sin = jnp.cos(ang), jnp.sin(ang)
    dim = jnp.arange(LANES) % HEAD_DIM
    idx = dim % half
    c = jnp.where(dim[None, :] < ROPE_DIM, cos[:, idx], 1.0)
    s_lo = jnp.where(dim[None, :] < half, -sin[:, idx], 0.0)
    s_hi = jnp.where((dim[None, :] >= half) & (dim[None, :] < ROPE_DIM), sin[:, idx], 0.0)
    return c.astype(F32), s_lo.astype(F32), s_hi.astype(F32)


def kernel(x, ffn1_norm, ffn1_w1, ffn1_w3, ffn1_w2, mix_norm, w_in, q_norm, k_norm,
           hg_lb_logits, hg_out_norm, w_out, ffn2_norm, ffn2_w1, ffn2_w3, ffn2_w2):
    b, s, d = x.shape
    depth = ffn1_norm.shape[0]
    tm = 512
    cos, s_lo, s_hi = _rope_lane_tables(s)
    head_of = jnp.arange(2 * LANES) // HEAD_DIM
    seg = (head_of[:, None] == head_of[None, :]).astype(BF16)
    for layer in range(depth):
        x1, (w_in16, w_out16, w1_16, w3_16, w2_16) = _ffn1(
            x.reshape(b * s, d), ffn1_norm[layer][None, :], ffn1_w1[layer].astype(BF16),
            ffn1_w3[layer].astype(BF16), ffn1_w2[layer].astype(BF16), FFN1_ROWS,
            [w_in[layer], w_out[layer], ffn2_w1[layer], ffn2_w3[layer], ffn2_w2[layer]])
        (q1, k1, v1, q4, k4, v4, q16, k16, v16, rec) = _mix_in(
            x1, mix_norm[layer][None, :], w_in16, seg,
            jnp.tile(q_norm[layer], ATTN_HEADS)[None, :], jnp.tile(k_norm[layer], ATTN_HEADS)[None, :],
            cos, s_lo, s_hi, hg_lb_logits, hg_out_norm[layer][None, :], b, tm, layer)
        os_ = [_attention(q, k, v) for q, k, v in
               ((q1[:, None], k1[:, None], v1[:, None]), (q4, k4, v4), (q16, k16, v16))]
        x = _out_ffn2(x1.reshape(b, s, d), os_, rec, w_out16,
                      ffn2_norm[layer][None, :], w1_16, w3_16, w2_16, tm)
    return x
```

```python
import functools

import jax
import jax.numpy as jnp
from jax import lax
from jax.experimental import pallas as pl
from jax.experimental.pallas import tpu as pltpu

F32 = jnp.float32
BF16 = jnp.bfloat16

EPS = 1e-6
NEG_INF = -1e30
LOG2_E = 1.4426950408889634
HEAD_DIM = 64
ATTN_HEADS = 8
ATTN_WIDTH = ATTN_HEADS * HEAD_DIM
ROPE_DIM = HEAD_DIM // 4
ROPE_THETA = 500000.0
DILATIONS = (1, 4, 16)
HG_HEADS = 4
HG_DIM = 128
HG_CHUNK = 64
HG_WIDTH = HG_HEADS * HG_DIM

LANES = 128
BF16_TILE_ROWS = 16
ATTN_BLOCK = 128
PAIRS = ATTN_WIDTH // LANES
ATTN_CHUNK = 8
ATTN_SLABS = PAIRS + 2
TWO_STAGE_STRIDE = 4
MIX_SUBTILES = 2
FFN_SUBTILE_ROWS = 256
FFN1_ROWS = 1024
VMEM_LIMIT = 56 * 1024 * 1024


def _const_spec(shape):
    nd = len(shape)
    return pl.BlockSpec(shape, lambda *_: (0,) * nd, pipeline_mode=pl.Buffered(1))


def _rms_norm_rows(x, w):
    return x * lax.rsqrt(jnp.mean(x * x, axis=-1, keepdims=True) + EPS) * w


def _silu(a):
    return a * (1.0 / (1.0 + jnp.exp(-a)))


def _swiglu_half_step(xs, nw, w1_ref, w3_ref, w2_ref):
    def finish(x, a, b):
        g = (_silu(a) * b).astype(BF16)
        return x + 0.5 * jnp.dot(g, w2_ref[...], preferred_element_type=F32)

    outs, pending = [], None
    for x in xs:
        h = _rms_norm_rows(x, nw).astype(BF16)
        a = jnp.dot(h, w1_ref[...], preferred_element_type=F32)
        b = jnp.dot(h, w3_ref[...], preferred_element_type=F32)
        if pending is not None:
            outs.append(finish(*pending))
        pending = (x, a, b)
    outs.append(finish(*pending))
    return outs


def _split_bf16(x):
    hi = x.astype(BF16)
    return hi, (x - hi.astype(F32)).astype(BF16)


def _row_subtiles(n):
    ts = FFN_SUBTILE_ROWS
    return [slice(i, i + ts) for i in range(0, n, ts)]


def _ffn1_kernel(*refs, cast_steps, nsteps):
    ncast = len(cast_steps)
    x_ref, nw_ref, w1_ref, w3_ref, w2_ref = refs[:5]
    cast_in = refs[5:5 + ncast]
    o_ref = refs[5 + ncast]
    cast_out = refs[6 + ncast:]
    step = pl.program_id(0)
    for src, dst, active in zip(cast_in, cast_out, cast_steps):
        if active == nsteps:
            dst[...] = src[...].astype(BF16)
        else:
            @pl.when(step < active)
            def _(src=src, dst=dst):
                dst[...] = src[...].astype(BF16)
    subs = _row_subtiles(x_ref.shape[0])
    outs = _swiglu_half_step([x_ref[r, :] for r in subs], nw_ref[...], w1_ref, w3_ref, w2_ref)
    for r, o in zip(subs, outs):
        o_ref[r, :] = o


def _cast_chunk_rows(rows, nsteps):
    for cr in range(BF16_TILE_ROWS, rows + 1, BF16_TILE_ROWS):
        if rows % cr == 0 and rows // cr <= nsteps:
            return cr
    raise ValueError(f"no bf16-aligned row chunk for {rows} rows in {nsteps} steps")


def _ffn1(x2d, nw, w1, w3, w2, tm, later_weights):
    n, d = x2d.shape
    f = w1.shape[1]
    nsteps = n // tm
    cast_specs, cast_shapes, cast_steps = [], [], []
    for w in later_weights:
        cr = _cast_chunk_rows(w.shape[0], nsteps)
        active = w.shape[0] // cr
        cast_specs.append(pl.BlockSpec((cr, w.shape[1]),
                                       lambda i, last=active - 1: (jnp.minimum(i, last), 0)))
        cast_shapes.append(jax.ShapeDtypeStruct(w.shape, BF16))
        cast_steps.append(active)
    outs = pl.pallas_call(
        functools.partial(_ffn1_kernel, cast_steps=tuple(cast_steps), nsteps=nsteps),
        grid=(nsteps,),
        in_specs=[
            pl.BlockSpec((tm, d), lambda i: (i, 0)),
            _const_spec((1, d)),
            _const_spec((d, f)),
            _const_spec((d, f)),
            _const_spec((f, d)),
        ] + cast_specs,
        out_specs=[pl.BlockSpec((tm, d), lambda i: (i, 0))] + cast_specs,
        out_shape=[jax.ShapeDtypeStruct((n, d), F32)] + cast_shapes,
        compiler_params=pltpu.CompilerParams(
            dimension_semantics=("arbitrary",), vmem_limit_bytes=VMEM_LIMIT),
        name="ffn1",
    )(x2d, nw, w1, w3, w2, *later_weights)
    return outs[0], outs[1:]


def _head_norm_rope(t, seg_ref, w, cos, sin_lo, sin_hi, scale):
    sq = (t * t).astype(BF16)
    seg = seg_ref[...]
    sw = seg.shape[0]
    ms = jnp.concatenate(
        [jnp.dot(sq[:, c:c + sw], seg, preferred_element_type=F32)
         for c in range(0, ATTN_WIDTH, sw)], axis=1) * (1.0 / HEAD_DIM)
    y = t * lax.rsqrt(ms + EPS) * w
    outs = []
    for c in range(PAIRS):
        yc = y[:, c * LANES:(c + 1) * LANES]
        from_hi = pltpu.roll(yc, LANES - ROPE_DIM // 2, axis=1)
        from_lo = pltpu.roll(yc, ROPE_DIM // 2, axis=1)
        outs.append((yc * cos + from_hi * sin_lo + from_lo * sin_hi) * scale)
    return outs


def _store_attn_input(slabs, ti, rows, sub, ts, outs, slab_scr, res4_scr):
    o1, o4, o16 = outs
    lanes = [slice(c * LANES, (c + 1) * LANES) for c in range(PAIRS)]
    for c in range(PAIRS):
        o1[0, rows, lanes[c]] = slabs[c].astype(BF16)
        slab_scr[sub, ti * PAIRS + c] = slabs[c]
    for c in range(PAIRS):
        sl = ti * PAIRS + c
        for r in range(4):
            rows4 = slab_scr[sub, sl, pl.ds(r, ts // 4, stride=4), :]
            o4[0, r, sub * (ts // 4):(sub + 1) * (ts // 4), lanes[c]] = rows4.astype(BF16)
            res4_scr[sub, sl, r] = rows4
    for c in range(PAIRS):
        sl = ti * PAIRS + c
        for r in range(4):
            for a_ in range(4):
                rows16 = res4_scr[sub, sl, r, pl.ds(a_, ts // 16, stride=4), :]
                o16[0, 4 * a_ + r, sub * (ts // 16):(sub + 1) * (ts // 16), lanes[c]] = (
                    rows16.astype(BF16))


def _hgrn_head_pieces(p_at, head, lb, nw, state_scr, seq_start, rec_ref, tm):
    c = HG_CHUNK
    t2 = 2 * c
    ngroup = tm // t2
    base = 3 * ATTN_WIDTH + head * HG_DIM
    col = lambda grp: base + grp * HG_WIDTH
    atts, qds, vs, entering, css, decs = [], [], [], [], [], []

    def masks():
        row = lax.broadcasted_iota(jnp.int32, (t2, HG_DIM), 0)
        return row % c, row < c

    def scan():
        state = jnp.where(seq_start, 0.0, state_scr[head])
        for ch in range(tm // c):
            entering.append(state.astype(BF16))
            state = state * decs[ch] + css[ch]
        state_scr[head] = state

    def score_group(t):
        rows = slice(t * t2, (t + 1) * t2)
        rin, first_chunk = masks()
        cc = lax.broadcasted_iota(jnp.int32, (t2, t2), 1)
        rr = lax.broadcasted_iota(jnp.int32, (t2, t2), 0)
        tril = ((rr // c) == (cc // c)) & ((cc % c) <= (rr % c))
        f = lb + (1.0 - lb) * (1.0 / (1.0 + jnp.exp(-p_at(rows, col(1), HG_DIM))))
        gl = jnp.log(f)
        kk = 1.0 - f
        for sft in (1, 2, 4, 8, 16, 32):
            gl = gl + jnp.where(rin >= sft, pltpu.roll(gl, sft, axis=0), 0.0)
        g_last = jnp.where(first_chunk, gl[c - 1:c, :], gl[t2 - 1:t2, :])
        qd = (p_at(rows, col(0), HG_DIM) * jnp.exp(gl)).astype(BF16)
        kd = (kk * jnp.exp(-gl)).astype(BF16)
        kl = (kk * jnp.exp(g_last - gl)).astype(BF16)
        v = p_at(rows, col(2), HG_DIM).astype(BF16)
        att = lax.dot_general(qd, kd, (((1,), (1,)), ((), ())), preferred_element_type=F32)
        atts.append(jnp.where(tril, att, 0.0).astype(BF16))
        zero = jnp.zeros_like(kl)
        kl2 = jnp.concatenate([jnp.where(first_chunk, kl, zero), jnp.where(first_chunk, zero, kl)],
                              axis=1)
        cs2 = lax.dot_general(v, kl2, (((0,), (0,)), ((), ())), preferred_element_type=F32)
        css.extend([cs2[:, :HG_DIM], cs2[:, HG_DIM:]])
        decs.extend([jnp.exp(gl[c - 1:c, :]), jnp.exp(gl[t2 - 1:t2, :])])
        qds.append(qd)
        vs.append(v)

    def output_group(t):
        rows = slice(t * t2, (t + 1) * t2)
        _, first_chunk = masks()
        intra = jnp.dot(atts[t], vs[t], preferred_element_type=F32)
        states = jnp.concatenate([entering[2 * t], entering[2 * t + 1]], axis=0)
        inter2 = lax.dot_general(qds[t], states, (((1,), (1,)), ((), ())),
                                 preferred_element_type=F32)
        rec = intra + jnp.where(first_chunk, inter2[:, :HG_DIM], inter2[:, HG_DIM:])
        gate = p_at(rows, col(3), HG_DIM)
        rec_ref[0, rows, head * HG_DIM:(head + 1) * HG_DIM] = (
            _rms_norm_rows(rec, nw) * _silu(gate)).astype(BF16)

    def scores():
        for t in range(ngroup):
            score_group(t)
        scan()

    def outputs():
        for t in range(ngroup):
            output_group(t)

    return [scores, outputs]


def _mix_in_kernel(x_ref, nw_ref, w_ref, seg_ref, qw_ref, kw_ref, cos_ref, slo_ref, shi_ref,
                   lbl_ref, hnw_ref,
                   q1_ref, k1_ref, v1_ref, q4_ref, k4_ref, v4_ref, q16_ref, k16_ref, v16_ref,
                   rec_ref, pa_scr, pb_scr, slab_scr, res4_scr, state_scr,
                   *, layer, tiles_per_seq):
    s = pl.program_id(0)
    tm = x_ref.shape[0]
    cols = w_ref.shape[1]
    nsub = slab_scr.shape[0]
    ts = tm // nsub

    @pl.when(s == 0)
    def _():
        pb_scr[...] = jnp.zeros(pb_scr.shape, F32)
        state_scr[...] = jnp.zeros(state_scr.shape, F32)

    def step(write_scr, read_scr):
        a = ATTN_WIDTH
        normed = {}

        pw = ATTN_WIDTH

        def project(sub, c0):
            rows = slice(sub * ts, (sub + 1) * ts)
            if sub not in normed:
                normed[sub] = _rms_norm_rows(x_ref[rows, :], nw_ref[...]).astype(BF16)
            write_scr[rows, c0:c0 + pw] = jnp.dot(normed[sub], w_ref[:, c0:c0 + pw],
                                                  preferred_element_type=F32)

        p_at = lambda rows, c0, width: read_scr[rows, c0:c0 + width]
        outs = ((q1_ref, q4_ref, q16_ref), (k1_ref, k4_ref, k16_ref), (v1_ref, v4_ref, v16_ref))

        def attn_input(sub, ti):
            rows = slice(sub * ts, (sub + 1) * ts)
            if ti == 2:
                slabs = [p_at(rows, 2 * a + c * LANES, LANES) for c in range(PAIRS)]
            else:
                gain, scale = ((qw_ref, LOG2_E * HEAD_DIM ** -0.5), (kw_ref, 1.0))[ti]
                slabs = _head_norm_rope(p_at(rows, ti * a, a), seg_ref, gain[...], cos_ref[rows, :],
                                        slo_ref[rows, :], shi_ref[rows, :], scale)
            _store_attn_input(slabs, ti, rows, sub, ts, outs[ti], slab_scr, res4_scr)

        lg = lbl_ref[...]
        e = jnp.exp(lg - jnp.max(lg, axis=0, keepdims=True))
        lb = jnp.sum(e[0:layer + 1, :], axis=0, keepdims=True) / jnp.sum(e, axis=0, keepdims=True)
        seq_start = (jnp.maximum(s - 1, 0) % tiles_per_seq) == 0

        attn_pieces = [functools.partial(attn_input, sub, ti)
                       for sub in range(nsub) for ti in range(3)]
        hgrn_pieces = []
        for head in range(HG_HEADS):
            hgrn_pieces += _hgrn_head_pieces(p_at, head, lb[:, head * HG_DIM:(head + 1) * HG_DIM],
                                             hnw_ref[...], state_scr, seq_start, rec_ref, tm)
        finishing = []
        for i in range(max(len(attn_pieces), len(hgrn_pieces))):
            finishing += hgrn_pieces[i:i + 1] + attn_pieces[i:i + 1]
        projecting = [functools.partial(project, sub, c0)
                      for sub in range(nsub) for c0 in range(0, cols, pw)]
        done_f = 0
        for i, piece in enumerate(projecting):
            piece()
            upto = (i + 1) * len(finishing) // len(projecting)
            for f_piece in finishing[done_f:upto]:
                f_piece()
            done_f = upto

    pl.when(s % 2 == 0)(lambda: step(pa_scr, pb_scr))
    pl.when(s % 2 == 1)(lambda: step(pb_scr, pa_scr))


def _mix_in(x2d, nw, w_in, seg, qw, kw, cos, slo, shi, lb_logits, hnw, batch, tm, layer):
    n, d = x2d.shape
    s = n // batch
    cols = w_in.shape[1]
    a = ATTN_WIDTH
    ntile = n // tm
    per_seq = s // tm
    ts = tm // MIX_SUBTILES
    done = lambda i: jnp.maximum(i - 1, 0)
    row = lambda i: (done(i) // per_seq, done(i) % per_seq, 0)
    out_specs, out_shape = [], []
    for dil in DILATIONS:
        for _ in range(3):
            if dil == 1:
                out_specs.append(pl.BlockSpec((1, tm, a), row))
                out_shape.append(jax.ShapeDtypeStruct((batch, s, a), BF16))
            else:
                out_specs.append(pl.BlockSpec(
                    (1, dil, tm // dil, a), lambda i: (done(i) // per_seq, 0, done(i) % per_seq, 0)))
                out_shape.append(jax.ShapeDtypeStruct((batch, dil, s // dil, a), BF16))
    out_specs.append(pl.BlockSpec((1, tm, HG_WIDTH), row))
    out_shape.append(jax.ShapeDtypeStruct((batch, s, HG_WIDTH), BF16))
    pos = lambda i: (done(i) % per_seq, 0)
    return pl.pallas_call(
        functools.partial(_mix_in_kernel, layer=layer, tiles_per_seq=per_seq),
        grid=(ntile + 1,),
        in_specs=[
            pl.BlockSpec((tm, d), lambda i: (jnp.minimum(i, ntile - 1), 0)),
            _const_spec((1, d)),
            _const_spec((d, cols)),
            _const_spec(seg.shape),
            _const_spec((1, a)),
            _const_spec((1, a)),
            pl.BlockSpec((tm, LANES), pos),
            pl.BlockSpec((tm, LANES), pos),
            pl.BlockSpec((tm, LANES), pos),
            _const_spec(lb_logits.shape),
            _const_spec((1, HG_DIM)),
        ],
        out_specs=out_specs,
        out_shape=out_shape,
        scratch_shapes=[pltpu.VMEM((tm, cols), F32),
                        pltpu.VMEM((tm, cols), F32),
                        pltpu.VMEM((MIX_SUBTILES, 3 * PAIRS, ts, LANES), F32),
                        pltpu.VMEM((MIX_SUBTILES, 3 * PAIRS, 4, ts // 4, LANES), F32),
                        pltpu.VMEM((HG_HEADS, HG_DIM, HG_DIM), F32)],
        compiler_params=pltpu.CompilerParams(
            dimension_semantics=("arbitrary",), vmem_limit_bytes=VMEM_LIMIT),
        name="mix_in",
    )(x2d, nw, w_in, seg, qw, kw, cos, slo, shi, lb_logits, hnw)


def _attn_kernel(q_ref, k_ref, v_ref, o_ref, p_scr, stage_scr=None, *, dilation, nblk):
    w = ATTN_BLOCK
    groups = dilation
    lane = lax.broadcasted_iota(jnp.int32, (w, LANES), 1)
    head0 = lane < HEAD_DIM
    qi1 = lax.broadcasted_iota(jnp.int32, (2 * w, w), 0) % w
    kj1 = lax.broadcasted_iota(jnp.int32, (2 * w, w), 1)
    valid_first = kj1 <= qi1
    qi2 = lax.broadcasted_iota(jnp.int32, (2 * w, 2 * w), 0) % w
    kj2 = lax.broadcasted_iota(jnp.int32, (2 * w, 2 * w), 1)
    dist = qi2 + w - kj2
    valid_band = (dist >= 0) & (dist <= w)

    def store_rows(slab, r0, g, val, slot):
        if dilation == 1:
            o_ref[0, slab, pl.ds(r0, w), :] = val
        elif slot is None:
            o_ref[0, slab, pl.ds(r0 * dilation + g, w, stride=dilation), :] = val
        else:
            plane, a = slot
            stage_scr[slab, plane, pl.ds(a, w, stride=TWO_STAGE_STRIDE), :] = val

    def score_block(g, qb, r0, rows_k, nk, valid, slot):
        rows_q = pl.ds(r0, w)
        m_acc = jnp.zeros((w, LANES), F32)
        for hp in range(PAIRS):
            lanes = slice(hp * LANES, (hp + 1) * LANES)
            q2 = q_ref[0, g, rows_q, lanes]
            zero = jnp.zeros_like(q2)
            qq = jnp.concatenate([jnp.where(head0, q2, zero), jnp.where(head0, zero, q2)], axis=0)
            s = lax.dot_general(qq, k_ref[0, g, rows_k, lanes], (((1,), (1,)), ((), ())),
                                preferred_element_type=F32)
            s = jnp.where(valid, s, NEG_INF)
            m = jnp.max(s, axis=1, keepdims=True)
            p_scr[g * nblk + qb, hp, :, 0:nk] = jnp.exp2(s - m).astype(BF16)
            for h in range(2):
                m_acc = jnp.where(lane == 2 * hp + h, m[h * w:(h + 1) * w], m_acc)
        store_rows(PAIRS, r0, g, m_acc, slot)

    def value_block(g, qb, r0, rows_k, nk, slot):
        d_acc = jnp.zeros((w, LANES), F32)
        ones = jnp.ones((nk, LANES), BF16)
        for hp in range(PAIRS):
            lanes = slice(hp * LANES, (hp + 1) * LANES)
            vx = jnp.concatenate([v_ref[0, g, rows_k, lanes], ones], axis=1)
            r = jnp.dot(p_scr[g * nblk + qb, hp, :, 0:nk], vx, preferred_element_type=F32)
            store_rows(hp, r0, g, jnp.where(head0, r[:w, :LANES], r[w:, :LANES]), slot)
            for h in range(2):
                d_acc = jnp.where(lane == 2 * hp + h, r[h * w:(h + 1) * w, LANES:], d_acc)
        store_rows(PAIRS + 1, r0, g, d_acc, slot)

    def run_chunk(blocks):
        args = []
        for g, qb, *rest in blocks:
            slot = rest[0] if rest else None
            if isinstance(qb, int) and qb == 0:
                args.append((g, 0, 0, pl.ds(0, w), w, valid_first, slot))
            else:
                r0 = qb * w if isinstance(qb, int) else pl.multiple_of(qb * w, w)
                args.append((g, qb, r0, pl.ds(r0 - w, 2 * w), 2 * w, valid_band, slot))
        for g, qb, r0, rows_k, nk, valid, slot in args:
            score_block(g, qb, r0, rows_k, nk, valid, slot)
        for g, qb, r0, rows_k, nk, _, slot in args:
            value_block(g, qb, r0, rows_k, nk, slot)

    def loop_chunks(lo, hi, blocks_of, after=None):
        def body(i, carry):
            run_chunk(blocks_of(i))
            if after is not None:
                after(i)
            return carry
        lax.fori_loop(lo, hi, body, 0)

    c = ATTN_CHUNK
    if stage_scr is not None:
        planes = stage_scr.shape[1]
        sub = dilation // TWO_STAGE_STRIDE

        def second_interleave(i):
            for plane in range(planes):
                for slab in range(ATTN_SLABS):
                    o_ref[0, slab, pl.ds(i * planes + plane, TWO_STAGE_STRIDE * w, stride=sub), :] = (
                        stage_scr[slab, plane])

        loop_chunks(0, sub // planes,
                    lambda i: [(i * planes + plane + sub * a, 0, (plane, a))
                               for plane in range(planes) for a in range(TWO_STAGE_STRIDE)],
                    after=second_interleave)
    elif nblk <= c:
        per = min(c // nblk, groups)
        loop_chunks(0, groups // per,
                    lambda i: [(i * per + g, qb) for g in range(per) for qb in range(nblk)])
    else:
        run_chunk([(0, qb) for qb in range(c)])
        loop_chunks(1, nblk // c, lambda i: [(0, i * c + j) for j in range(c)])


def _attention(q, k, v):
    b, dilation, l, a = q.shape
    s = l * dilation
    nblk = l // ATTN_BLOCK
    spec = pl.BlockSpec((1, dilation, l, a), lambda i: (i, 0, 0, 0))
    scratch = [pltpu.VMEM((dilation * nblk, PAIRS, 2 * ATTN_BLOCK, 2 * ATTN_BLOCK), BF16)]
    if nblk == 1 and dilation == TWO_STAGE_STRIDE ** 2:
        scratch.append(pltpu.VMEM((ATTN_SLABS, ATTN_CHUNK // TWO_STAGE_STRIDE,
                                   TWO_STAGE_STRIDE * ATTN_BLOCK, LANES), F32))
    return pl.pallas_call(
        functools.partial(_attn_kernel, dilation=dilation, nblk=nblk),
        grid=(b,),
        in_specs=[spec, spec, spec],
        out_specs=pl.BlockSpec((1, ATTN_SLABS, s, LANES), lambda i: (i, 0, 0, 0)),
        out_shape=jax.ShapeDtypeStruct((b, ATTN_SLABS, s, LANES), F32),
        scratch_shapes=scratch,
        compiler_params=pltpu.CompilerParams(
            dimension_semantics=("arbitrary",), vmem_limit_bytes=VMEM_LIMIT),
        name=f"attn_d{dilation}",
    )(q, k, v)


def _out_ffn2_kernel(x_ref, o1_ref, o2_ref, o3_ref, rec_ref,
                     wo_ref, nw_ref, w1_ref, w3_ref, w2_ref, out_ref):
    o_refs = (o1_ref, o2_ref, o3_ref)
    ms = [r[0, PAIRS, :, 0:ATTN_HEADS] for r in o_refs]
    dens = [r[0, PAIRS + 1, :, 0:ATTN_HEADS] for r in o_refs]
    mx = jnp.maximum(jnp.maximum(ms[0], ms[1]), ms[2])
    es = [jnp.exp2(m - mx) for m in ms]
    inv = 1.0 / (es[0] * dens[0] + es[1] * dens[1] + es[2] * dens[2])
    halves = [h for e in es for h in _split_bf16(e * inv)]
    stacked = jnp.concatenate(halves, axis=1)
    kdim = stacked.shape[1]
    src = lax.broadcasted_iota(jnp.int32, (kdim, len(es) * ATTN_WIDTH), 0)
    dst = lax.broadcasted_iota(jnp.int32, (kdim, len(es) * ATTN_WIDTH), 1)
    expand = ((src // (2 * ATTN_HEADS) == dst // ATTN_WIDTH)
              & (src % ATTN_HEADS == (dst % ATTN_WIDTH) // HEAD_DIM)).astype(BF16)
    wide = jnp.dot(stacked, expand, preferred_element_type=F32)
    parts = []
    for hp in range(PAIRS):
        acc = None
        for p in range(len(es)):
            lo_lane = p * ATTN_WIDTH + hp * LANES
            term = wide[:, lo_lane:lo_lane + LANES] * o_refs[p][0, hp]
            acc = term if acc is None else acc + term
        parts.append(acc.astype(BF16))
    mixed = jnp.concatenate(parts + [rec_ref[0]], axis=1)
    x2 = x_ref[0] + jnp.dot(mixed, wo_ref[...], preferred_element_type=F32)
    subs = _row_subtiles(x2.shape[0])
    outs = _swiglu_half_step([x2[r, :] for r in subs], nw_ref[...], w1_ref, w3_ref, w2_ref)
    for r, o in zip(subs, outs):
        out_ref[0, r, :] = o


def _out_ffn2(x3d, os_, rec, wo, nw, w1, w3, w2, tm):
    b, s, d = x3d.shape
    f = w1.shape[1]
    a = ATTN_WIDTH
    row = lambda i, j: (i, j, 0)
    return pl.pallas_call(
        _out_ffn2_kernel,
        grid=(b, s // tm),
        in_specs=[pl.BlockSpec((1, tm, d), row)]
                 + [pl.BlockSpec((1, ATTN_SLABS, tm, LANES), lambda i, j: (i, 0, j, 0))] * 3
                 + [pl.BlockSpec((1, tm, HG_WIDTH), row),
                    _const_spec((a + HG_WIDTH, d)),
                    _const_spec((1, d)),
                    _const_spec((d, f)),
                    _const_spec((d, f)),
                    _const_spec((f, d))],
        out_specs=pl.BlockSpec((1, tm, d), row),
        out_shape=jax.ShapeDtypeStruct((b, s, d), F32),
        compiler_params=pltpu.CompilerParams(
            dimension_semantics=("arbitrary", "arbitrary"), vmem_limit_bytes=VMEM_LIMIT),
        name="out_ffn2",
    )(x3d, *os_, rec, wo, nw, w1, w3, w2)


def _rope_lane_tables(s):
    half = ROPE_DIM // 2
    inv = ROPE_THETA ** (-jnp.arange(0, ROPE_DIM, 2, dtype=F32) / ROPE_DIM)
    ang = jnp.arange(s, dtype=F32)[:, None] * inv[None, :]
    cos, sin = jnp.cos(ang), jnp.sin(ang)
    dim = jnp.arange(LANES) % HEAD_DIM
    idx = dim % half
    c = jnp.where(dim[None, :] < ROPE_DIM, cos[:, idx], 1.0)
    s_lo = jnp.where(dim[None, :] < half, -sin[:, idx], 0.0)
    s_hi = jnp.where((dim[None, :] >= half) & (dim[None, :] < ROPE_DIM), sin[:, idx], 0.0)
    return c.astype(F32), s_lo.astype(F32), s_hi.astype(F32)


def kernel(x, ffn1_norm, ffn1_w1, ffn1_w3, ffn1_w2, mix_norm, w_in, q_norm, k_norm,
           hg_lb_logits, hg_out_norm, w_out, ffn2_norm, ffn2_w1, ffn2_w3, ffn2_w2):
    b, s, d = x.shape
    depth = ffn1_norm.shape[0]
    tm = 512
    cos, s_lo, s_hi = _rope_lane_tables(s)
    head_of = jnp.arange(2 * LANES) // HEAD_DIM
    seg = (head_of[:, None] == head_of[None, :]).astype(BF16)
    for layer in range(depth):
        x1, (w_in16, w_out16, w1_16, w3_16, w2_16) = _ffn1(
            x.reshape(b * s, d), ffn1_norm[layer][None, :], ffn1_w1[layer].astype(BF16),
            ffn1_w3[layer].astype(BF16), ffn1_w2[layer].astype(BF16), FFN1_ROWS,
            [w_in[layer], w_out[layer], ffn2_w1[layer], ffn2_w3[layer], ffn2_w2[layer]])
        (q1, k1, v1, q4, k4, v4, q16, k16, v16, rec) = _mix_in(
            x1, mix_norm[layer][None, :], w_in16, seg,
            jnp.tile(q_norm[layer], ATTN_HEADS)[None, :], jnp.tile(k_norm[layer], ATTN_HEADS)[None, :],
            cos, s_lo, s_hi, hg_lb_logits, hg_out_norm[layer][None, :], b, tm, layer)
        os_ = [_attention(q, k, v) for q, k, v in
               ((q1[:, None], k1[:, None], v1[:, None]), (q4, k4, v4), (q16, k16, v16))]
        x = _out_ffn2(x1.reshape(b, s, d), os_, rec, w_out16,
                      ffn2_norm[layer][None, :], w1_16, w3_16, w2_16, tm)
    return x
```

```python
import functools

import jax
import jax.numpy as jnp
from jax import lax
from jax.experimental import pallas as pl
from jax.experimental.pallas import tpu as pltpu

F32 = jnp.float32
BF16 = jnp.bfloat16

EPS = 1e-6
NEG_INF = -1e30
LOG2_E = 1.4426950408889634
HEAD_DIM = 64
ATTN_HEADS = 8
ATTN_WIDTH = ATTN_HEADS * HEAD_DIM
ROPE_DIM = HEAD_DIM // 4
ROPE_THETA = 500000.0
DILATIONS = (1, 4, 16)
HG_HEADS = 4
HG_DIM = 128
HG_CHUNK = 64
HG_WIDTH = HG_HEADS * HG_DIM

LANES = 128
BF16_TILE_ROWS = 16
ATTN_BLOCK = 128
PAIRS = ATTN_WIDTH // LANES
ATTN_CHUNK = 8
ATTN_SLABS = PAIRS + 2
TWO_STAGE_STRIDE = 4
MIX_SUBTILES = 2
TILE_ROWS = 512
FFN_SUBTILE_ROWS = 256
FFN1_ROWS = 1024
VMEM_LIMIT = 56 * 1024 * 1024


def _const_spec(shape):
    nd = len(shape)
    return pl.BlockSpec(shape, lambda *_: (0,) * nd, pipeline_mode=pl.Buffered(1))


def _rms_norm_rows(x, w):
    return x * lax.rsqrt(jnp.mean(x * x, axis=-1, keepdims=True) + EPS) * w


def _silu(a):
    return a * (1.0 / (1.0 + jnp.exp(-a)))


def _swiglu_half_step(xs, nw, w1_ref, w3_ref, w2_ref):
    def finish(x, a, b):
        g = (_silu(a) * b).astype(BF16)
        return x + 0.5 * jnp.dot(g, w2_ref[...], preferred_element_type=F32)

    outs, pending = [], None
    for x in xs:
        h = _rms_norm_rows(x, nw).astype(BF16)
        a = jnp.dot(h, w1_ref[...], preferred_element_type=F32)
        b = jnp.dot(h, w3_ref[...], preferred_element_type=F32)
        if pending is not None:
            outs.append(finish(*pending))
        pending = (x, a, b)
    outs.append(finish(*pending))
    return outs


def _split_bf16(x):
    hi = x.astype(BF16)
    return hi, (x - hi.astype(F32)).astype(BF16)


def _row_subtiles(n):
    ts = FFN_SUBTILE_ROWS
    return [slice(i, i + ts) for i in range(0, n, ts)]


def _ffn1_kernel(*refs, cast_steps, nsteps):
    ncast = len(cast_steps)
    x_ref, nw_ref, w1_ref, w3_ref, w2_ref = refs[:5]
    cast_in = refs[5:5 + ncast]
    o_ref = refs[5 + ncast]
    cast_out = refs[6 + ncast:]
    step = pl.program_id(0)
    for src, dst, active in zip(cast_in, cast_out, cast_steps):
        if active == nsteps:
            dst[...] = src[...].astype(BF16)
        else:
            @pl.when(step < active)
            def _(src=src, dst=dst):
                dst[...] = src[...].astype(BF16)
    subs = _row_subtiles(x_ref.shape[0])
    outs = _swiglu_half_step([x_ref[r, :] for r in subs], nw_ref[...], w1_ref, w3_ref, w2_ref)
    for r, o in zip(subs, outs):
        o_ref[r, :] = o


def _cast_chunk_rows(rows, nsteps):
    for cr in range(BF16_TILE_ROWS, rows + 1, BF16_TILE_ROWS):
        if rows % cr == 0 and rows // cr <= nsteps:
            return cr
    raise ValueError(f"no bf16-aligned row chunk for {rows} rows in {nsteps} steps")


def _ffn1(x2d, nw, w1, w3, w2, tm, later_weights):
    n, d = x2d.shape
    f = w1.shape[1]
    nsteps = n // tm
    cast_specs, cast_shapes, cast_steps = [], [], []
    for w in later_weights:
        cr = _cast_chunk_rows(w.shape[0], nsteps)
        active = w.shape[0] // cr
        cast_specs.append(pl.BlockSpec((cr, w.shape[1]),
                                       lambda i, last=active - 1: (jnp.minimum(i, last), 0)))
        cast_shapes.append(jax.ShapeDtypeStruct(w.shape, BF16))
        cast_steps.append(active)
    outs = pl.pallas_call(
        functools.partial(_ffn1_kernel, cast_steps=tuple(cast_steps), nsteps=nsteps),
        grid=(nsteps,),
        in_specs=[
            pl.BlockSpec((tm, d), lambda i: (i, 0)),
            _const_spec((1, d)),
            _const_spec((d, f)),
            _const_spec((d, f)),
            _const_spec((f, d)),
        ] + cast_specs,
        out_specs=[pl.BlockSpec((tm, d), lambda i: (i, 0))] + cast_specs,
        out_shape=[jax.ShapeDtypeStruct((n, d), F32)] + cast_shapes,
        compiler_params=pltpu.CompilerParams(
            dimension_semantics=("arbitrary",), vmem_limit_bytes=VMEM_LIMIT),
        name="ffn1",
    )(x2d, nw, w1, w3, w2, *later_weights)
    return outs[0], outs[1:]


def _head_norm_rope(t, seg_ref, w, cos, sin_lo, sin_hi, scale):
    sq = (t * t).astype(BF16)
    seg = seg_ref[...]
    sw = seg.shape[0]
    ms = jnp.concatenate(
        [jnp.dot(sq[:, c:c + sw], seg, preferred_element_type=F32)
         for c in range(0, ATTN_WIDTH, sw)], axis=1) * (1.0 / HEAD_DIM)
    y = t * lax.rsqrt(ms + EPS) * w
    outs = []
    for c in range(PAIRS):
        yc = y[:, c * LANES:(c + 1) * LANES]
        from_hi = pltpu.roll(yc, LANES - ROPE_DIM // 2, axis=1)
        from_lo = pltpu.roll(yc, ROPE_DIM // 2, axis=1)
        outs.append((yc * cos + from_hi * sin_lo + from_lo * sin_hi) * scale)
    return outs


def _store_attn_input(slabs, ti, rows, sub, ts, outs, slab_scr, res4_scr):
    o1, o4, o16 = outs
    lanes = [slice(ti * ATTN_WIDTH + c * LANES, ti * ATTN_WIDTH + (c + 1) * LANES)
             for c in range(PAIRS)]
    for c in range(PAIRS):
        o1[0, rows, lanes[c]] = slabs[c].astype(BF16)
        slab_scr[sub, ti * PAIRS + c] = slabs[c]
    for c in range(PAIRS):
        sl = ti * PAIRS + c
        for r in range(4):
            rows4 = slab_scr[sub, sl, pl.ds(r, ts // 4, stride=4), :]
            o4[0, r, sub * (ts // 4):(sub + 1) * (ts // 4), lanes[c]] = rows4.astype(BF16)
            res4_scr[sub, sl, r] = rows4
    for c in range(PAIRS):
        sl = ti * PAIRS + c
        for r in range(4):
            for a_ in range(4):
                rows16 = res4_scr[sub, sl, r, pl.ds(a_, ts // 16, stride=4), :]
                o16[0, 4 * a_ + r, sub * (ts // 16):(sub + 1) * (ts // 16), lanes[c]] = (
                    rows16.astype(BF16))


def _hgrn_head_pieces(p_at, head, lb, nw, state_scr, seq_start, rec_ref, tm):
    c = HG_CHUNK
    t2 = 2 * c
    ngroup = tm // t2
    base = 3 * ATTN_WIDTH + head * HG_DIM
    col = lambda grp: base + grp * HG_WIDTH
    atts, qds, vs, entering, css, decs = [], [], [], [], [], []

    def masks():
        row = lax.broadcasted_iota(jnp.int32, (t2, HG_DIM), 0)
        return row % c, row < c

    def scan():
        state = jnp.where(seq_start, 0.0, state_scr[head])
        for ch in range(tm // c):
            entering.append(state.astype(BF16))
            state = state * decs[ch] + css[ch]
        state_scr[head] = state

    def score_group(t):
        rows = slice(t * t2, (t + 1) * t2)
        rin, first_chunk = masks()
        cc = lax.broadcasted_iota(jnp.int32, (t2, t2), 1)
        rr = lax.broadcasted_iota(jnp.int32, (t2, t2), 0)
        tril = ((rr // c) == (cc // c)) & ((cc % c) <= (rr % c))
        f = lb + (1.0 - lb) * (1.0 / (1.0 + jnp.exp(-p_at(rows, col(1), HG_DIM))))
        gl = jnp.log(f)
        kk = 1.0 - f
        for sft in (1, 2, 4, 8, 16, 32):
            gl = gl + jnp.where(rin >= sft, pltpu.roll(gl, sft, axis=0), 0.0)
        g_last = jnp.where(first_chunk, gl[c - 1:c, :], gl[t2 - 1:t2, :])
        qd = (p_at(rows, col(0), HG_DIM) * jnp.exp(gl)).astype(BF16)
        kd = (kk * jnp.exp(-gl)).astype(BF16)
        kl = (kk * jnp.exp(g_last - gl)).astype(BF16)
        v = p_at(rows, col(2), HG_DIM).astype(BF16)
        att = lax.dot_general(qd, kd, (((1,), (1,)), ((), ())), preferred_element_type=F32)
        atts.append(jnp.where(tril, att, 0.0).astype(BF16))
        zero = jnp.zeros_like(kl)
        kl2 = jnp.concatenate([jnp.where(first_chunk, kl, zero), jnp.where(first_chunk, zero, kl)],
                              axis=1)
        cs2 = lax.dot_general(v, kl2, (((0,), (0,)), ((), ())), preferred_element_type=F32)
        css.extend([cs2[:, :HG_DIM], cs2[:, HG_DIM:]])
        decs.extend([jnp.exp(gl[c - 1:c, :]), jnp.exp(gl[t2 - 1:t2, :])])
        qds.append(qd)
        vs.append(v)

    def output_group(t):
        rows = slice(t * t2, (t + 1) * t2)
        _, first_chunk = masks()
        intra = jnp.dot(atts[t], vs[t], preferred_element_type=F32)
        states = jnp.concatenate([entering[2 * t], entering[2 * t + 1]], axis=0)
        inter2 = lax.dot_general(qds[t], states, (((1,), (1,)), ((), ())),
                                 preferred_element_type=F32)
        rec = intra + jnp.where(first_chunk, inter2[:, :HG_DIM], inter2[:, HG_DIM:])
        gate = p_at(rows, col(3), HG_DIM)
        rec_ref[0, rows, head * HG_DIM:(head + 1) * HG_DIM] = (
            _rms_norm_rows(rec, nw) * _silu(gate)).astype(BF16)

    def scores():
        for t in range(ngroup):
            score_group(t)
        scan()

    def outputs():
        for t in range(ngroup):
            output_group(t)

    return [scores, outputs]


def _mix_in_kernel(x_ref, nw_ref, w_ref, seg_ref, qw_ref, kw_ref, cos_ref, slo_ref, shi_ref,
                   lbl_ref, hnw_ref,
                   qkv1_ref, qkv4_ref, qkv16_ref,
                   rec_ref, pa_scr, pb_scr, slab_scr, res4_scr, state_scr,
                   *, layer, tiles_per_seq):
    s = pl.program_id(0)
    tm = x_ref.shape[0]
    cols = w_ref.shape[1]
    nsub = slab_scr.shape[0]
    ts = tm // nsub

    @pl.when(s == 0)
    def _():
        pb_scr[...] = jnp.zeros(pb_scr.shape, F32)
        state_scr[...] = jnp.zeros(state_scr.shape, F32)

    def step(write_scr, read_scr):
        a = ATTN_WIDTH
        normed = {}

        pw = ATTN_WIDTH

        def project(sub, c0):
            rows = slice(sub * ts, (sub + 1) * ts)
            if sub not in normed:
                normed[sub] = _rms_norm_rows(x_ref[rows, :], nw_ref[...]).astype(BF16)
            write_scr[rows, c0:c0 + pw] = jnp.dot(normed[sub], w_ref[:, c0:c0 + pw],
                                                  preferred_element_type=F32)

        p_at = lambda rows, c0, width: read_scr[rows, c0:c0 + width]
        outs = (qkv1_ref, qkv4_ref, qkv16_ref)

        def attn_input(sub, ti):
            rows = slice(sub * ts, (sub + 1) * ts)
            if ti == 2:
                slabs = [p_at(rows, 2 * a + c * LANES, LANES) for c in range(PAIRS)]
            else:
                gain, scale = ((qw_ref, LOG2_E * HEAD_DIM ** -0.5), (kw_ref, 1.0))[ti]
                slabs = _head_norm_rope(p_at(rows, ti * a, a), seg_ref, gain[...], cos_ref[rows, :],
                                        slo_ref[rows, :], shi_ref[rows, :], scale)
            _store_attn_input(slabs, ti, rows, sub, ts, outs, slab_scr, res4_scr)

        lg = lbl_ref[...]
        e = jnp.exp(lg - jnp.max(lg, axis=0, keepdims=True))
        lb = jnp.sum(e[0:layer + 1, :], axis=0, keepdims=True) / jnp.sum(e, axis=0, keepdims=True)
        seq_start = (jnp.maximum(s - 1, 0) % tiles_per_seq) == 0

        attn_pieces = [functools.partial(attn_input, sub, ti)
                       for sub in range(nsub) for ti in range(3)]
        hgrn_pieces = []
        for head in range(HG_HEADS):
            hgrn_pieces += _hgrn_head_pieces(p_at, head, lb[:, head * HG_DIM:(head + 1) * HG_DIM],
                                             hnw_ref[...], state_scr, seq_start, rec_ref, tm)
        finishing = []
        for i in range(max(len(attn_pieces), len(hgrn_pieces))):
            finishing += hgrn_pieces[i:i + 1] + attn_pieces[i:i + 1]
        projecting = [functools.partial(project, sub, c0)
                      for sub in range(nsub) for c0 in range(0, cols, pw)]
        done_f = 0
        for i, piece in enumerate(projecting):
            piece()
            upto = (i + 1) * len(finishing) // len(projecting)
            for f_piece in finishing[done_f:upto]:
                f_piece()
            done_f = upto

    pl.when(s % 2 == 0)(lambda: step(pa_scr, pb_scr))
    pl.when(s % 2 == 1)(lambda: step(pb_scr, pa_scr))


def _mix_in(x2d, nw, w_in, seg, qw, kw, cos, slo, shi, lb_logits, hnw, batch, tm, layer):
    n, d = x2d.shape
    s = n // batch
    cols = w_in.shape[1]
    a = ATTN_WIDTH
    ntile = n // tm
    per_seq = s // tm
    ts = tm // MIX_SUBTILES
    done = lambda i: jnp.maximum(i - 1, 0)
    row = lambda i: (done(i) // per_seq, done(i) % per_seq, 0)
    out_specs, out_shape = [], []
    for dil in DILATIONS:
        if dil == 1:
            out_specs.append(pl.BlockSpec((1, tm, 3 * a), row))
            out_shape.append(jax.ShapeDtypeStruct((batch, s, 3 * a), BF16))
        else:
            out_specs.append(pl.BlockSpec(
                (1, dil, tm // dil, 3 * a), lambda i: (done(i) // per_seq, 0, done(i) % per_seq, 0)))
            out_shape.append(jax.ShapeDtypeStruct((batch, dil, s // dil, 3 * a), BF16))
    out_specs.append(pl.BlockSpec((1, tm, HG_WIDTH), row))
    out_shape.append(jax.ShapeDtypeStruct((batch, s, HG_WIDTH), BF16))
    pos = lambda i: (done(i) % per_seq, 0)
    return pl.pallas_call(
        functools.partial(_mix_in_kernel, layer=layer, tiles_per_seq=per_seq),
        grid=(ntile + 1,),
        in_specs=[
            pl.BlockSpec((tm, d), lambda i: (jnp.minimum(i, ntile - 1), 0)),
            _const_spec((1, d)),
            _const_spec((d, cols)),
            _const_spec(seg.shape),
            _const_spec((1, a)),
            _const_spec((1, a)),
            pl.BlockSpec((tm, LANES), pos),
            pl.BlockSpec((tm, LANES), pos),
            pl.BlockSpec((tm, LANES), pos),
            _const_spec(lb_logits.shape),
            _const_spec((1, HG_DIM)),
        ],
        out_specs=out_specs,
        out_shape=out_shape,
        scratch_shapes=[pltpu.VMEM((tm, cols), F32),
                        pltpu.VMEM((tm, cols), F32),
                        pltpu.VMEM((MIX_SUBTILES, 3 * PAIRS, ts, LANES), F32),
                        pltpu.VMEM((MIX_SUBTILES, 3 * PAIRS, 4, ts // 4, LANES), F32),
                        pltpu.VMEM((HG_HEADS, HG_DIM, HG_DIM), F32)],
        compiler_params=pltpu.CompilerParams(
            dimension_semantics=("arbitrary",), vmem_limit_bytes=VMEM_LIMIT),
        name="mix_in",
    )(x2d, nw, w_in, seg, qw, kw, cos, slo, shi, lb_logits, hnw)


def _attn_kernel(qkv_ref, o_ref, p_scr, stage_scr=None, *, dilation, nblk):
    w = ATTN_BLOCK
    groups = dilation
    lane = lax.broadcasted_iota(jnp.int32, (w, LANES), 1)
    head0 = lane < HEAD_DIM
    qi1 = lax.broadcasted_iota(jnp.int32, (2 * w, w), 0) % w
    kj1 = lax.broadcasted_iota(jnp.int32, (2 * w, w), 1)
    valid_first = kj1 <= qi1
    qi2 = lax.broadcasted_iota(jnp.int32, (2 * w, 2 * w), 0) % w
    kj2 = lax.broadcasted_iota(jnp.int32, (2 * w, 2 * w), 1)
    dist = qi2 + w - kj2
    valid_band = (dist >= 0) & (dist <= w)

    def store_rows(slab, r0, g, val, slot):
        if dilation == 1:
            o_ref[0, slab, pl.ds(r0, w), :] = val
        elif slot is None:
            o_ref[0, slab, pl.ds(r0 * dilation + g, w, stride=dilation), :] = val
        else:
            plane, a = slot
            stage_scr[slab, plane, pl.ds(a, w, stride=TWO_STAGE_STRIDE), :] = val

    def score_block(g, qb, r0, rows_k, nk, valid, slot):
        rows_q = pl.ds(r0, w)
        m_acc = jnp.zeros((w, LANES), F32)
        for hp in range(PAIRS):
            lanes = slice(hp * LANES, (hp + 1) * LANES)
            q2 = qkv_ref[0, g, rows_q, lanes]
            zero = jnp.zeros_like(q2)
            qq = jnp.concatenate([jnp.where(head0, q2, zero), jnp.where(head0, zero, q2)], axis=0)
            k_lanes = slice(ATTN_WIDTH + hp * LANES, ATTN_WIDTH + (hp + 1) * LANES)
            s = lax.dot_general(qq, qkv_ref[0, g, rows_k, k_lanes], (((1,), (1,)), ((), ())),
                                preferred_element_type=F32)
            s = jnp.where(valid, s, NEG_INF)
            m = jnp.max(s, axis=1, keepdims=True)
            p_scr[g * nblk + qb, hp, :, 0:nk] = jnp.exp2(s - m).astype(BF16)
            for h in range(2):
                m_acc = jnp.where(lane == 2 * hp + h, m[h * w:(h + 1) * w], m_acc)
        store_rows(PAIRS, r0, g, m_acc, slot)

    def value_block(g, qb, r0, rows_k, nk, slot):
        d_acc = jnp.zeros((w, LANES), F32)
        ones = jnp.ones((nk, LANES), BF16)
        for hp in range(PAIRS):
            v_lanes = slice(2 * ATTN_WIDTH + hp * LANES, 2 * ATTN_WIDTH + (hp + 1) * LANES)
            vx = jnp.concatenate([qkv_ref[0, g, rows_k, v_lanes], ones], axis=1)
            r = jnp.dot(p_scr[g * nblk + qb, hp, :, 0:nk], vx, preferred_element_type=F32)
            store_rows(hp, r0, g, jnp.where(head0, r[:w, :LANES], r[w:, :LANES]), slot)
            for h in range(2):
                d_acc = jnp.where(lane == 2 * hp + h, r[h * w:(h + 1) * w, LANES:], d_acc)
        store_rows(PAIRS + 1, r0, g, d_acc, slot)

    def run_chunk(blocks):
        args = []
        for g, qb, *rest in blocks:
            slot = rest[0] if rest else None
            if isinstance(qb, int) and qb == 0:
                args.append((g, 0, 0, pl.ds(0, w), w, valid_first, slot))
            else:
                r0 = qb * w if isinstance(qb, int) else pl.multiple_of(qb * w, w)
                args.append((g, qb, r0, pl.ds(r0 - w, 2 * w), 2 * w, valid_band, slot))
        for g, qb, r0, rows_k, nk, valid, slot in args:
            score_block(g, qb, r0, rows_k, nk, valid, slot)
        for g, qb, r0, rows_k, nk, _, slot in args:
            value_block(g, qb, r0, rows_k, nk, slot)

    def loop_chunks(lo, hi, blocks_of, after=None):
        def body(i, carry):
            run_chunk(blocks_of(i))
            if after is not None:
                after(i)
            return carry
        lax.fori_loop(lo, hi, body, 0)

    c = ATTN_CHUNK
    if stage_scr is not None:
        planes = stage_scr.shape[1]
        sub = dilation // TWO_STAGE_STRIDE

        def second_interleave(i):
            for plane in range(planes):
                for slab in range(ATTN_SLABS):
                    o_ref[0, slab, pl.ds(i * planes + plane, TWO_STAGE_STRIDE * w, stride=sub), :] = (
                        stage_scr[slab, plane])

        loop_chunks(0, sub // planes,
                    lambda i: [(i * planes + plane + sub * a, 0, (plane, a))
                               for plane in range(planes) for a in range(TWO_STAGE_STRIDE)],
                    after=second_interleave)
    elif nblk <= c:
        per = min(c // nblk, groups)
        loop_chunks(0, groups // per,
                    lambda i: [(i * per + g, qb) for g in range(per) for qb in range(nblk)])
    else:
        run_chunk([(0, qb) for qb in range(c)])
        loop_chunks(1, nblk // c, lambda i: [(0, i * c + j) for j in range(c)])


def _attention(qkv):
    b, dilation, l, a3 = qkv.shape
    s = l * dilation
    nblk = l // ATTN_BLOCK
    spec = pl.BlockSpec((1, dilation, l, a3), lambda i: (i, 0, 0, 0))
    scratch = [pltpu.VMEM((dilation * nblk, PAIRS, 2 * ATTN_BLOCK, 2 * ATTN_BLOCK), BF16)]
    if nblk == 1 and dilation == TWO_STAGE_STRIDE ** 2:
        scratch.append(pltpu.VMEM((ATTN_SLABS, ATTN_CHUNK // TWO_STAGE_STRIDE,
                                   TWO_STAGE_STRIDE * ATTN_BLOCK, LANES), F32))
    return pl.pallas_call(
        functools.partial(_attn_kernel, dilation=dilation, nblk=nblk),
        grid=(b,),
        in_specs=[spec],
        out_specs=pl.BlockSpec((1, ATTN_SLABS, s, LANES), lambda i: (i, 0, 0, 0)),
        out_shape=jax.ShapeDtypeStruct((b, ATTN_SLABS, s, LANES), F32),
        scratch_shapes=scratch,
        compiler_params=pltpu.CompilerParams(
            dimension_semantics=("arbitrary",), vmem_limit_bytes=VMEM_LIMIT),
        name=f"attn_d{dilation}",
    )(qkv)


def _out_ffn2_kernel(x_ref, o1_ref, o2_ref, o3_ref, rec_ref,
                     wo_ref, nw_ref, w1_ref, w3_ref, w2_ref, out_ref):
    o_refs = (o1_ref, o2_ref, o3_ref)
    ms = [r[0, PAIRS, :, 0:ATTN_HEADS] for r in o_refs]
    dens = [r[0, PAIRS + 1, :, 0:ATTN_HEADS] for r in o_refs]
    mx = jnp.maximum(jnp.maximum(ms[0], ms[1]), ms[2])
    es = [jnp.exp2(m - mx) for m in ms]
    inv = 1.0 / (es[0] * dens[0] + es[1] * dens[1] + es[2] * dens[2])
    halves = [h for e in es for h in _split_bf16(e * inv)]
    stacked = jnp.concatenate(halves, axis=1)
    kdim = stacked.shape[1]
    src = lax.broadcasted_iota(jnp.int32, (kdim, len(es) * ATTN_WIDTH), 0)
    dst = lax.broadcasted_iota(jnp.int32, (kdim, len(es) * ATTN_WIDTH), 1)
    expand = ((src // (2 * ATTN_HEADS) == dst // ATTN_WIDTH)
              & (src % ATTN_HEADS == (dst % ATTN_WIDTH) // HEAD_DIM)).astype(BF16)
    wide = jnp.dot(stacked, expand, preferred_element_type=F32)
    parts = []
    for hp in range(PAIRS):
        acc = None
        for p in range(len(es)):
            lo_lane = p * ATTN_WIDTH + hp * LANES
            term = wide[:, lo_lane:lo_lane + LANES] * o_refs[p][0, hp]
            acc = term if acc is None else acc + term
        parts.append(acc.astype(BF16))
    mixed = jnp.concatenate(parts + [rec_ref[0]], axis=1)
    x2 = x_ref[0] + jnp.dot(mixed, wo_ref[...], preferred_element_type=F32)
    subs = _row_subtiles(x2.shape[0])
    outs = _swiglu_half_step([x2[r, :] for r in subs], nw_ref[...], w1_ref, w3_ref, w2_ref)
    for r, o in zip(subs, outs):
        out_ref[0, r, :] = o


def _out_ffn2(x3d, os_, rec, wo, nw, w1, w3, w2, tm):
    b, s, d = x3d.shape
    f = w1.shape[1]
    a = ATTN_WIDTH
    row = lambda i, j: (i, j, 0)
    return pl.pallas_call(
        _out_ffn2_kernel,
        grid=(b, s // tm),
        in_specs=[pl.BlockSpec((1, tm, d), row)]
                 + [pl.BlockSpec((1, ATTN_SLABS, tm, LANES), lambda i, j: (i, 0, j, 0))] * 3
                 + [pl.BlockSpec((1, tm, HG_WIDTH), row),
                    _const_spec((a + HG_WIDTH, d)),
                    _const_spec((1, d)),
                    _const_spec((d, f)),
                    _const_spec((d, f)),
                    _const_spec((f, d))],
        out_specs=pl.BlockSpec((1, tm, d), row),
        out_shape=jax.ShapeDtypeStruct((b, s, d), F32),
        compiler_params=pltpu.CompilerParams(
            dimension_semantics=("arbitrary", "arbitrary"), vmem_limit_bytes=VMEM_LIMIT),
        name="out_ffn2",
    )(x3d, *os_, rec, wo, nw, w1, w3, w2)


def _rope_lane_tables(s):
    half = ROPE_DIM // 2
    inv = ROPE_THETA ** (-jnp.arange(0, ROPE_DIM, 2, dtype=F32) / ROPE_DIM)
    ang = jnp.arange(s, dtype=F32)[:, None] * inv[None, :]
    cos, sin = jnp.cos(ang), jnp.sin(ang)
    dim = jnp.arange(LANES) % HEAD_DIM
    idx = dim % half
    c = jnp.where(dim[None, :] < ROPE_DIM, cos[:, idx], 1.0)
    s_lo = jnp.where(dim[None, :] < half, -sin[:, idx], 0.0)
    s_hi = jnp.where((dim[None, :] >= half) & (dim[None, :] < ROPE_DIM), sin[:, idx], 0.0)
    return c.astype(F32), s_lo.astype(F32), s_hi.astype(F32)


def kernel(x, ffn1_norm, ffn1_w1, ffn1_w3, ffn1_w2, mix_norm, w_in, q_norm, k_norm,
           hg_lb_logits, hg_out_norm, w_out, ffn2_norm, ffn2_w1, ffn2_w3, ffn2_w2):
    b, s, d = x.shape
    depth = ffn1_norm.shape[0]
    cos, s_lo, s_hi = _rope_lane_tables(s)
    head_of = jnp.arange(2 * LANES) // HEAD_DIM
    seg = (head_of[:, None] == head_of[None, :]).astype(BF16)
    for layer in range(depth):
        x1, (w_in16, w_out16, w1_16, w3_16, w2_16) = _ffn1(
            x.reshape(b * s, d), ffn1_norm[layer][None, :], ffn1_w1[layer].astype(BF16),
            ffn1_w3[layer].astype(BF16), ffn1_w2[layer].astype(BF16), FFN1_ROWS,
            [w_in[layer], w_out[layer], ffn2_w1[layer], ffn2_w3[layer], ffn2_w2[layer]])
        qkv1, qkv4, qkv16, rec = _mix_in(
            x1, mix_norm[layer][None, :], w_in16, seg,
            jnp.tile(q_norm[layer], ATTN_HEADS)[None, :], jnp.tile(k_norm[layer], ATTN_HEADS)[None, :],
            cos, s_lo, s_hi, hg_lb_logits, hg_out_norm[layer][None, :], b, TILE_ROWS, layer)
        os_ = [_attention(qkv) for qkv in (qkv1[:, None], qkv4, qkv16)]
        x = _out_ffn2(x1.reshape(b, s, d), os_, rec, w_out16,
                      ffn2_norm[layer][None, :], w1_16, w3_16, w2_16, TILE_ROWS)
    return x
```
